```python
import jax, jax.numpy as jnp
from jax import lax
import numpy as np

D_MODEL = 2048
BATCH = 2
SEQ = 8192
DEPTH = 2

N_MIXERS = 2
N_CONF = (DEPTH + 1) // 2
N_SSM = DEPTH // 2

CONV_KERNEL = 31

SSM_EXPAND = 2
D_INNER = SSM_EXPAND * D_MODEL
SSM_HEAD_DIM = 64
SSM_HEADS = D_INNER // SSM_HEAD_DIM
SSM_GROUPS = 8
HEADS_PER_GROUP = SSM_HEADS // SSM_GROUPS
SSM_STATE = 128
SSM_CONV = 4
SSM_CHUNK = 256
GN = SSM_GROUPS * SSM_STATE
CONV_DIM = D_INNER + 2 * GN
D_IN_PROJ = D_INNER + CONV_DIM + SSM_HEADS

D_FF = 7168
N_EXPERTS = 8
TOP_K = 2

EPS = 1e-6

kernel_name = "hybrid_conformer_conv_mamba2_moe"


def rmsnorm(x, g):
    xf = x.astype(jnp.float32)
    y = xf * lax.rsqrt(jnp.mean(xf * xf, axis=-1, keepdims=True) + EPS)
    return (y * g.astype(jnp.float32)).astype(x.dtype)


def causal_depthwise_conv(x, w, b):
    k = w.shape[0]
    y = lax.conv_general_dilated(
        x, w[:, None, :], window_strides=(1,), padding=[(k - 1, 0)],
        dimension_numbers=("NWC", "WIO", "NWC"), feature_group_count=x.shape[-1])
    return y + b


def conformer_conv(u, w_pw1, b_pw1, w_dw, b_dw, ln_g, ln_b, w_pw2, b_pw2):
    h = u @ w_pw1 + b_pw1
    val, gate = jnp.split(h, 2, axis=-1)
    h = val * jax.nn.sigmoid(gate)
    h = causal_depthwise_conv(h, w_dw, b_dw)
    hf = h.astype(jnp.float32)
    mu = jnp.mean(hf, axis=-1, keepdims=True)
    var = jnp.mean(jnp.square(hf - mu), axis=-1, keepdims=True)
    hf = (hf - mu) * lax.rsqrt(var + EPS) * ln_g.astype(jnp.float32) + ln_b.astype(jnp.float32)
    h = jax.nn.silu(hf).astype(u.dtype)
    return h @ w_pw2 + b_pw2


def ssd_chunked(x, a, b, c):
    bsz, seq = x.shape[0], x.shape[1]
    pad = (-seq) % SSM_CHUNK
    if pad:
        x = jnp.pad(x, ((0, 0), (0, pad), (0, 0), (0, 0)))
        a = jnp.pad(a, ((0, 0), (0, pad), (0, 0)))
        b = jnp.pad(b, ((0, 0), (0, pad), (0, 0), (0, 0)))
        c = jnp.pad(c, ((0, 0), (0, pad), (0, 0), (0, 0)))
    n_chunks = (seq + pad) // SSM_CHUNK
    x = x.reshape(bsz, n_chunks, SSM_CHUNK, SSM_GROUPS, HEADS_PER_GROUP, SSM_HEAD_DIM)
    a = a.reshape(bsz, n_chunks, SSM_CHUNK, SSM_GROUPS, HEADS_PER_GROUP).astype(jnp.float32)
    b = b.reshape(bsz, n_chunks, SSM_CHUNK, SSM_GROUPS, SSM_STATE)
    c = c.reshape(bsz, n_chunks, SSM_CHUNK, SSM_GROUPS, SSM_STATE)
    a_cs = jnp.cumsum(a, axis=2)
    causal = jnp.tril(jnp.ones((SSM_CHUNK, SSM_CHUNK), dtype=bool))[:, :, None, None]

    def intra_chunk(args):
        x_k, acs_k, b_k, c_k = args
        seg = acs_k[:, :, None] - acs_k[:, None, :]
        decay = jnp.exp(jnp.where(causal, seg, -jnp.inf))
        cb = jnp.einsum("blgn,bsgn->blsg", c_k, b_k)
        return jnp.einsum("blsg,blsgh,bsghp->blghp", cb, decay, x_k)

    y_diag = lax.map(intra_chunk, (x.swapaxes(0, 1), a_cs.swapaxes(0, 1),
                                   b.swapaxes(0, 1), c.swapaxes(0, 1))).swapaxes(0, 1)

    decay_states = jnp.exp(a_cs[:, :, -1:] - a_cs)
    states = jnp.einsum("bclgn,bclgh,bclghp->bcghpn", b, decay_states, x)
    chunk_decay = jnp.exp(a_cs[:, :, -1])

    def carry_step(state, inp):
        st_k, dec_k = inp
        return state * dec_k[..., None, None] + st_k, state

    init = jnp.zeros_like(states[:, 0])
    _, prev = lax.scan(carry_step, init, (states.swapaxes(0, 1), chunk_decay.swapaxes(0, 1)))
    prev = prev.swapaxes(0, 1)
    y_off = jnp.einsum("bclgn,bcghpn,bclgh->bclghp", c, prev, jnp.exp(a_cs))
    y = (y_diag + y_off).reshape(bsz, seq + pad, SSM_HEADS, SSM_HEAD_DIM)
    return y[:, :seq]


def mamba2_mixer(u, w_in, conv_w, conv_b, dt_bias, a_log, d_skip, norm_g, w_out):
    bsz, seq, _ = u.shape
    zxbcdt = u @ w_in
    z = zxbcdt[..., :D_INNER]
    xbc = zxbcdt[..., D_INNER:D_INNER + CONV_DIM]
    dt_raw = zxbcdt[..., D_INNER + CONV_DIM:]
    xbc = jax.nn.silu(causal_depthwise_conv(xbc, conv_w, conv_b))
    xs = xbc[..., :D_INNER].reshape(bsz, seq, SSM_HEADS, SSM_HEAD_DIM).astype(jnp.float32)
    bm = xbc[..., D_INNER:D_INNER + GN].reshape(bsz, seq, SSM_GROUPS, SSM_STATE)
    cm = xbc[..., D_INNER + GN:].reshape(bsz, seq, SSM_GROUPS, SSM_STATE)
    dt = jax.nn.softplus(dt_raw.astype(jnp.float32) + dt_bias.astype(jnp.float32))
    a = -jnp.exp(a_log.astype(jnp.float32))
    y = ssd_chunked(xs * dt[..., None], dt * a, bm, cm)
    y = y + xs * d_skip.astype(jnp.float32)[:, None]
    y = y.reshape(bsz, seq, D_INNER) * jax.nn.silu(z.astype(jnp.float32))
    yg = y.reshape(bsz, seq, SSM_GROUPS, D_INNER // SSM_GROUPS)
    yg = yg * lax.rsqrt(jnp.mean(yg * yg, axis=-1, keepdims=True) + EPS)
    y = yg.reshape(bsz, seq, D_INNER) * norm_g.astype(jnp.float32)
    return y.astype(u.dtype) @ w_out


def swiglu(u, w_gate, w_up, w_down):
    return (jax.nn.silu(u @ w_gate) * (u @ w_up)) @ w_down


def moe_swiglu(u, w_router, w_gate, w_up, w_down):
    logits = (u @ w_router).astype(jnp.float32)
    top_vals, top_idx = lax.top_k(logits, TOP_K)
    top_w = jax.nn.softmax(top_vals, axis=-1)
    gates = jnp.sum(jax.nn.one_hot(top_idx, N_EXPERTS, dtype=jnp.float32) * top_w[..., None], axis=-2)
    gates = gates.astype(u.dtype)
    out = jnp.zeros_like(u)
    for e in range(N_EXPERTS):
        out = out + swiglu(u, w_gate[e], w_up[e], w_down[e]) * gates[..., e:e + 1]
    return out


def setup_inputs(seed: int = 0) -> dict:
    key = jax.random.key(seed)
    ks = iter(jax.random.split(key, 40))

    def nrm(shape, scale):
        return jax.random.normal(next(ks), shape, jnp.float32) * scale

    def gain(shape):
        return 1.0 + 0.02 * jax.random.normal(next(ks), shape, jnp.float32)

    dt = jnp.exp(jax.random.uniform(next(ks), (N_SSM, SSM_HEADS), jnp.float32,
                                    np.log(1e-3), np.log(1e-1)))
    dt_bias = dt + jnp.log(-jnp.expm1(-dt))
    a_log = jnp.log(jax.random.uniform(next(ks), (N_SSM, SSM_HEADS), jnp.float32, 1.0, 16.0))

    return {
        "x": nrm((BATCH, SEQ, D_MODEL), 1.0),
        "norm_mix_g": gain((DEPTH, D_MODEL)),
        "norm_ffn_g": gain((DEPTH, D_MODEL)),
        "final_norm_g": gain((D_MODEL,)),
        "conf_pw1_w": nrm((N_CONF, D_MODEL, 2 * D_MODEL), D_MODEL ** -0.5),
        "conf_pw1_b": nrm((N_CONF, 2 * D_MODEL), 0.02),
        "conf_dw_w": nrm((N_CONF, CONV_KERNEL, D_MODEL), CONV_KERNEL ** -0.5),
        "conf_dw_b": nrm((N_CONF, D_MODEL), 0.02),
        "conf_ln_g": gain((N_CONF, D_MODEL)),
        "conf_ln_b": nrm((N_CONF, D_MODEL), 0.02),
        "conf_pw2_w": nrm((N_CONF, D_MODEL, D_MODEL), D_MODEL ** -0.5),
        "conf_pw2_b": nrm((N_CONF, D_MODEL), 0.02),
        "ssm_in_w": nrm((N_SSM, D_MODEL, D_IN_PROJ), D_MODEL ** -0.5),
        "ssm_conv_w": nrm((N_SSM, SSM_CONV, CONV_DIM), SSM_CONV ** -0.5),
        "ssm_conv_b": nrm((N_SSM, CONV_DIM), 0.02),
        "ssm_dt_bias": dt_bias,
        "ssm_a_log": a_log,
        "ssm_d": gain((N_SSM, SSM_HEADS)),
        "ssm_norm_g": gain((N_SSM, D_INNER)),
        "ssm_out_w": nrm((N_SSM, D_INNER, D_MODEL), D_INNER ** -0.5),
        "ffn_w_gate": nrm((N_CONF, D_MODEL, D_FF), D_MODEL ** -0.5),
        "ffn_w_up": nrm((N_CONF, D_MODEL, D_FF), D_MODEL ** -0.5),
        "ffn_w_down": nrm((N_CONF, D_FF, D_MODEL), D_FF ** -0.5),
        "moe_router_w": nrm((N_SSM, D_MODEL, N_EXPERTS), D_MODEL ** -0.5),
        "moe_w_gate": nrm((N_SSM, N_EXPERTS, D_MODEL, D_FF), D_MODEL ** -0.5),
        "moe_w_up": nrm((N_SSM, N_EXPERTS, D_MODEL, D_FF), D_MODEL ** -0.5),
        "moe_w_down": nrm((N_SSM, N_EXPERTS, D_FF, D_MODEL), D_FF ** -0.5),
    }


def reference(x, norm_mix_g, norm_ffn_g, final_norm_g,
              conf_pw1_w, conf_pw1_b, conf_dw_w, conf_dw_b, conf_ln_g, conf_ln_b,
              conf_pw2_w, conf_pw2_b,
              ssm_in_w, ssm_conv_w, ssm_conv_b, ssm_dt_bias, ssm_a_log, ssm_d,
              ssm_norm_g, ssm_out_w,
              ffn_w_gate, ffn_w_up, ffn_w_down,
              moe_router_w, moe_w_gate, moe_w_up, moe_w_down):
    for i in range(DEPTH):
        j = i // N_MIXERS
        h = rmsnorm(x, norm_mix_g[i])
        if i % N_MIXERS == 0:
            x = x + conformer_conv(h, conf_pw1_w[j], conf_pw1_b[j], conf_dw_w[j], conf_dw_b[j],
                                   conf_ln_g[j], conf_ln_b[j], conf_pw2_w[j], conf_pw2_b[j])
        else:
            x = x + mamba2_mixer(h, ssm_in_w[j], ssm_conv_w[j], ssm_conv_b[j], ssm_dt_bias[j],
                                 ssm_a_log[j], ssm_d[j], ssm_norm_g[j], ssm_out_w[j])
        h = rmsnorm(x, norm_ffn_g[i])
        if i % 2 == 0:
            x = x + swiglu(h, ffn_w_gate[i // 2], ffn_w_up[i // 2], ffn_w_down[i // 2])
        else:
            x = x + moe_swiglu(h, moe_router_w[i // 2], moe_w_gate[i // 2],
                               moe_w_up[i // 2], moe_w_down[i // 2])
    return rmsnorm(x, final_norm_g)
```

```python
import functools

import jax
import jax.numpy as jnp
from jax import lax
from jax.experimental import pallas as pl
from jax.experimental.pallas import tpu as pltpu

EPS = 1e-6
SSM_GROUPS = 8
SSM_CHUNK = 256
TOP_K = 2
LANES = 128
SUBLANES = 8
VMEM_LIMIT = 56 << 20

F32 = jnp.float32
BF16 = jnp.bfloat16


def _cparams(*sem):
    return pltpu.CompilerParams(dimension_semantics=sem, vmem_limit_bytes=VMEM_LIMIT)


def _sigmoid(v):
    return 1.0 / (1.0 + jnp.exp(-v))


def _silu(v):
    return v * _sigmoid(v)


def _rms(x, g):
    return x * lax.rsqrt(jnp.mean(x * x, axis=-1, keepdims=True) + EPS) * g


def _dot(a, b):
    return jnp.dot(a, b, preferred_element_type=F32)


def _dot_nt(a, b):
    return lax.dot_general(a, b, (((1,), (1,)), ((), ())), preferred_element_type=F32)


def _blk(n, want):
    b = min(n, want)
    assert n % b == 0, (n, want)
    return b


def _norm_glu_kernel(x_ref, g_ref, wa_ref, wb_ref, ba_ref, bb_ref, o_ref, h_ref):
    @pl.when(pl.program_id(1) == 0)
    def _():
        h_ref[...] = _rms(x_ref[...], g_ref[...]).astype(BF16)

    h = h_ref[...]
    a = _dot(h, wa_ref[...]) + ba_ref[...]
    b = _dot(h, wb_ref[...]) + bb_ref[...]
    o_ref[...] = (a * _sigmoid(b)).astype(o_ref.dtype)


def _norm_glu(x, g, w, b, out_dtype):
    m, k = x.shape
    n = w.shape[1] // 2
    bm, bn = _blk(m, 1024), _blk(n, 512)
    nb = n // bn
    return pl.pallas_call(
        _norm_glu_kernel,
        out_shape=jax.ShapeDtypeStruct((m, n), out_dtype),
        grid=(m // bm, nb),
        in_specs=[
            pl.BlockSpec((bm, k), lambda i, j: (i, 0)),
            pl.BlockSpec((1, k), lambda i, j: (0, 0)),
            pl.BlockSpec((k, bn), lambda i, j: (0, j)),
            pl.BlockSpec((k, bn), lambda i, j: (0, j + nb)),
            pl.BlockSpec((1, bn), lambda i, j: (0, j)),
            pl.BlockSpec((1, bn), lambda i, j: (0, j + nb)),
        ],
        out_specs=pl.BlockSpec((bm, bn), lambda i, j: (i, j)),
        scratch_shapes=[pltpu.VMEM((bm, k), BF16)],
        compiler_params=_cparams("parallel", "arbitrary"),
        name="norm_glu",
    )(x, g, w, w, b, b)


def _norm_swiglu_kernel(x_ref, g_ref, wg_ref, wu_ref, o_ref, h_ref):
    @pl.when(pl.program_id(1) == 0)
    def _():
        h_ref[...] = _rms(x_ref[...], g_ref[...]).astype(BF16)

    h = h_ref[...]
    a = _dot(h, wg_ref[...])
    b = _dot(h, wu_ref[...])
    o_ref[...] = (_silu(a) * b).astype(o_ref.dtype)


def _norm_swiglu(x, g, wg, wu):
    m, k = x.shape
    n = wg.shape[1]
    bm, bn = _blk(m, 1024), _blk(n, 512)
    return pl.pallas_call(
        _norm_swiglu_kernel,
        out_shape=jax.ShapeDtypeStruct((m, n), BF16),
        grid=(m // bm, n // bn),
        in_specs=[
            pl.BlockSpec((bm, k), lambda i, j: (i, 0)),
            pl.BlockSpec((1, k), lambda i, j: (0, 0)),
            pl.BlockSpec((k, bn), lambda i, j: (0, j)),
            pl.BlockSpec((k, bn), lambda i, j: (0, j)),
        ],
        out_specs=pl.BlockSpec((bm, bn), lambda i, j: (i, j)),
        scratch_shapes=[pltpu.VMEM((bm, k), BF16)],
        compiler_params=_cparams("parallel", "arbitrary"),
        name="norm_swiglu",
    )(x, g, wg, wu)


def _norm_inproj_kernel(x_ref, g_ref, w_ref, wdt_ref, o_ref, dt_ref, h_ref):
    @pl.when(pl.program_id(1) == 0)
    def _():
        h = _rms(x_ref[...], g_ref[...]).astype(BF16)
        h_ref[...] = h
        dt_ref[...] = _dot(h, wdt_ref[...])

    o_ref[...] = _dot(h_ref[...], w_ref[...])


def _norm_inproj(x, g, w, wdt):
    m, k = x.shape
    n = w.shape[1]
    bm, bn = _blk(m, 1024), _blk(n, 512)
    return pl.pallas_call(
        _norm_inproj_kernel,
        out_shape=(jax.ShapeDtypeStruct((m, n), F32),
                   jax.ShapeDtypeStruct((m, wdt.shape[1]), F32)),
        grid=(m // bm, n // bn),
        in_specs=[
            pl.BlockSpec((bm, k), lambda i, j: (i, 0)),
            pl.BlockSpec((1, k), lambda i, j: (0, 0)),
            pl.BlockSpec((k, bn), lambda i, j: (0, j)),
            pl.BlockSpec((k, wdt.shape[1]), lambda i, j: (0, 0)),
        ],
        out_specs=(pl.BlockSpec((bm, bn), lambda i, j: (i, j)),
                   pl.BlockSpec((bm, wdt.shape[1]), lambda i, j: (i, 0))),
        scratch_shapes=[pltpu.VMEM((bm, k), BF16)],
        compiler_params=_cparams("parallel", "arbitrary"),
        name="norm_inproj",
    )(x, g, w, wdt)


def _mm_res_kernel(a_ref, w_ref, b_ref, r_ref, o_ref):
    o_ref[...] = r_ref[...] + _dot(a_ref[...], w_ref[...]) + b_ref[...]


def _mm_res(a, w, bias, res):
    m, k = a.shape
    n = w.shape[1]
    bm = _blk(m, 1024)
    bn = _blk(n, 256 if k > 4096 else 512)
    return pl.pallas_call(
        _mm_res_kernel,
        out_shape=jax.ShapeDtypeStruct((m, n), F32),
        grid=(m // bm, n // bn),
        in_specs=[
            pl.BlockSpec((bm, k), lambda i, j: (i, 0)),
            pl.BlockSpec((k, bn), lambda i, j: (0, j)),
            pl.BlockSpec((1, bn), lambda i, j: (0, j)),
            pl.BlockSpec((bm, bn), lambda i, j: (i, j)),
        ],
        out_specs=pl.BlockSpec((bm, bn), lambda i, j: (i, j)),
        compiler_params=_cparams("parallel", "arbitrary"),
        name="mm_res",
    )(a, w, bias, res)


CONV_ROWS = 32


def _dwconv_kernel(*refs, taps, halo, bt, cb, layer_norm):
    if layer_norm:
        xc_ref, xh_ref, w_ref, b_ref, lg_ref, lb_ref, o_ref, win_ref, y_ref = refs
    else:
        xc_ref, xh_ref, w_ref, b_ref, o_ref, win_ref = refs
        y_ref = None
    first = pl.program_id(1) == 0
    win_ref[0, 0:halo, :] = jnp.where(first, 0.0, xh_ref[...])
    win_ref[0, halo:halo + bt, :] = xc_ref[...]
    off = halo - (taps - 1)
    n_sh = halo + bt - SUBLANES
    for s in sorted({(off + j) % SUBLANES for j in range(taps)} - {0}):
        win_ref[s, 0:n_sh, :] = win_ref[0, s:s + n_sh, :]

    def body(r, carry):
        r0 = pl.multiple_of(r * CONV_ROWS, CONV_ROWS)
        for c in range(cb // LANES):
            cs = slice(c * LANES, (c + 1) * LANES)
            acc = jnp.broadcast_to(b_ref[:, cs], (CONV_ROWS, LANES))
            for j in range(taps):
                al, s = divmod(off + j, SUBLANES)
                xw = win_ref[s, pl.ds(r0 + al * SUBLANES, CONV_ROWS), cs]
                acc = acc + xw * w_ref[j:j + 1, cs]
            if layer_norm:
                y_ref[pl.ds(r0, CONV_ROWS), cs] = acc
            else:
                o_ref[pl.ds(r0, CONV_ROWS), cs] = _silu(acc).astype(o_ref.dtype)
        return carry

    lax.fori_loop(0, bt // CONV_ROWS, body, 0)

    if layer_norm:
        y = y_ref[...]
        mu = jnp.mean(y, axis=-1, keepdims=True)
        d = y - mu
        var = jnp.mean(d * d, axis=-1, keepdims=True)
        hn = d * lax.rsqrt(var + EPS) * lg_ref[...] + lb_ref[...]
        o_ref[...] = _silu(hn).astype(o_ref.dtype)


def _dwconv(x3, w, b, ln, out_dtype, c_start=0):
    bsz, seq, _ = x3.shape
    taps, ch = w.shape
    halo = SUBLANES * (-(-(taps - 1) // SUBLANES))
    bt = _blk(seq, 256)
    assert bt % halo == 0 and bt % CONV_ROWS == 0
    layer_norm = ln is not None
    cb = ch if layer_norm else _blk(ch, 1024)
    assert c_start % cb == 0
    co = c_start // cb
    hb = bt // halo
    in_specs = [
        pl.BlockSpec((None, bt, cb), lambda bi, i, c: (bi, i, c + co)),
        pl.BlockSpec((None, halo, cb),
                     lambda bi, i, c: (bi, jnp.maximum(i * hb - 1, 0), c + co)),
        pl.BlockSpec((taps, cb), lambda bi, i, c: (0, c)),
        pl.BlockSpec((1, cb), lambda bi, i, c: (0, c)),
    ]
    args = [x3, x3, w, b]
    scratch = [pltpu.VMEM((SUBLANES, halo + bt, cb), F32)]
    if layer_norm:
        in_specs += [pl.BlockSpec((1, cb), lambda bi, i, c: (0, c))] * 2
        args += [ln[0], ln[1]]
        scratch.append(pltpu.VMEM((bt, cb), F32))
    return pl.pallas_call(
        functools.partial(_dwconv_kernel, taps=taps, halo=halo, bt=bt, cb=cb,
                          layer_norm=layer_norm),
        out_shape=jax.ShapeDtypeStruct((bsz, seq, ch), out_dtype),
        grid=(bsz, seq // bt, ch // cb),
        in_specs=in_specs,
        out_specs=pl.BlockSpec((None, bt, cb), lambda bi, i, c: (bi, i, c)),
        scratch_shapes=scratch,
        compiler_params=_cparams("parallel", "parallel", "parallel"),
        name="dwconv_ln" if layer_norm else "dwconv",
    )(*args)


def _split3(v):
    hi = v.astype(BF16)
    r = v - hi.astype(F32)
    mid = r.astype(BF16)
    lo = (r - mid.astype(F32)).astype(BF16)
    return hi, mid, lo


def _expand_heads(v, hpg, hd):
    rows = v.shape[0]
    lane_head = lax.broadcasted_iota(jnp.int32, (rows, hpg * hd), 1) // hd
    out = jnp.zeros((rows, hpg * hd), F32)
    for h in range(hpg):
        out = jnp.where(lane_head == h, v[:, h:h + 1], out)
    return out


def _ssd_kernel(x_ref, b_ref, c_ref, z_ref, dtt_ref, dtb_ref, alog_ref, dexp_ref, ng_ref,
                o_ref, s_ref, y_ref, *, hpg, hd, q):
    @pl.when(pl.program_id(2) == 0)
    def _():
        s_ref[...] = jnp.zeros_like(s_ref)

    xg = x_ref[...]
    bg = b_ref[...]
    cg = c_ref[...]
    cgb = cg.astype(BF16)
    v = dtt_ref[...] + dtb_ref[...]
    dt = jnp.maximum(v, 0.0) + jnp.log1p(jnp.exp(-jnp.abs(v)))
    a = dt * (-jnp.exp(alog_ref[...]))

    a_pad = jnp.concatenate([a, jnp.zeros((LANES - hpg, q), F32)], axis=0)
    rr = lax.broadcasted_iota(jnp.int32, (q, q), 0)
    cc = lax.broadcasted_iota(jnp.int32, (q, q), 1)
    causal = rr >= cc
    tri_ls = jnp.where(causal, 1.0, 0.0).astype(BF16)
    tri_sl = jnp.where(rr <= cc, 1.0, 0.0).astype(BF16)
    acs_t = jnp.zeros((LANES, q), F32)
    acs = jnp.zeros((q, LANES), F32)
    for piece in _split3(a_pad):
        acs_t = acs_t + _dot(piece, tri_sl)
        acs = acs + _dot_nt(tri_ls, piece)
    acs_t = acs_t[0:hpg, :]
    wst_t = jnp.exp(acs_t[:, q - 1:q] - acs_t) * dt

    cb = _dot_nt(cgb, bg.astype(BF16))
    bt = bg.T
    lane = lax.broadcasted_iota(jnp.int32, (q, LANES), 1)
    pair = LANES // hd
    s_old = s_ref[...]
    for t in range(hpg // pair):
        ts = slice(t * LANES, (t + 1) * LANES)
        x_t = xg[:, ts]
        y_t = jnp.zeros((q, LANES), F32)
        s_t = jnp.zeros((bt.shape[0], LANES), F32)
        for p in range(pair):
            h = t * pair + p
            seg = acs[:, h:h + 1] - acs_t[h:h + 1, :]
            decay = jnp.exp(jnp.where(causal, seg, -jnp.inf))
            m = (cb * decay * dt[h:h + 1, :]).astype(BF16)
            xm = jnp.where((lane >= p * hd) & (lane < (p + 1) * hd), x_t, 0.0).astype(BF16)
            y_t = y_t + _dot(m, xm)
            bw = (bt * wst_t[h:h + 1, :]).astype(BF16)
            s_t = s_t + _dot(bw, xm)
        y_ref[:, ts] = y_t
        dec_t = _expand_heads(jnp.exp(acs[q - 1:q, t * pair:(t + 1) * pair]), pair, hd)
        s_ref[:, ts] = s_old[:, ts] * dec_t + s_t

    y = y_ref[...]
    y = y + _dot(cgb, s_old.astype(BF16)) * _expand_heads(jnp.exp(acs[:, 0:hpg]), hpg, hd)
    y = y + xg * dexp_ref[...]
    y = y * _silu(z_ref[...])
    y = y * lax.rsqrt(jnp.mean(y * y, axis=-1, keepdims=True) + EPS) * ng_ref[...]
    o_ref[...] = y.astype(o_ref.dtype)


def _ssd(zx, xbc, dtt, dt_bias, a_log, d_exp, norm_g, *, bsz, seq, d_inner, n_state):
    groups = SSM_GROUPS
    heads = dtt.shape[0]
    hpg = heads // groups
    hd = d_inner // heads
    gw = hpg * hd
    q = _blk(seq, SSM_CHUNK)
    nc = seq // q
    assert gw % LANES == 0 and LANES % hd == 0 and n_state % LANES == 0
    assert d_inner // groups == gw
    b_off = d_inner // n_state
    c_off = b_off + groups
    row = lambda b, g, c: b * nc + c
    return pl.pallas_call(
        functools.partial(_ssd_kernel, hpg=hpg, hd=hd, q=q),
        out_shape=jax.ShapeDtypeStruct((bsz * seq, d_inner), BF16),
        grid=(bsz, groups, nc),
        in_specs=[
            pl.BlockSpec((q, gw), lambda b, g, c: (row(b, g, c), g)),
            pl.BlockSpec((q, n_state), lambda b, g, c: (row(b, g, c), b_off + g)),
            pl.BlockSpec((q, n_state), lambda b, g, c: (row(b, g, c), c_off + g)),
            pl.BlockSpec((q, gw), lambda b, g, c: (row(b, g, c), g)),
            pl.BlockSpec((hpg, q), lambda b, g, c: (g, row(b, g, c))),
            pl.BlockSpec((hpg, 1), lambda b, g, c: (g, 0)),
            pl.BlockSpec((hpg, 1), lambda b, g, c: (g, 0)),
            pl.BlockSpec((1, gw), lambda b, g, c: (0, g)),
            pl.BlockSpec((1, gw), lambda b, g, c: (0, g)),
        ],
        out_specs=pl.BlockSpec((q, gw), lambda b, g, c: (row(b, g, c), g)),
        scratch_shapes=[pltpu.VMEM((n_state, gw), F32), pltpu.VMEM((q, gw), F32)],
        compiler_params=_cparams("parallel", "parallel", "arbitrary"),
        name="ssd",
    )(xbc, xbc, xbc, zx, dtt, dt_bias, a_log, d_exp, norm_g)


def _router_kernel(x_ref, g_ref, wr_ref, h_ref, idx_ref, gw_ref, *, n_experts):
    h = _rms(x_ref[...], g_ref[...])
    h_ref[...] = h
    logits = jnp.dot(h, wr_ref[...], preferred_element_type=F32,
                     precision=lax.Precision.HIGHEST)
    lane = lax.broadcasted_iota(jnp.int32, logits.shape, 1).astype(F32)
    neg = -jnp.inf
    lg = jnp.where(lane < n_experts, logits, neg)
    m1 = jnp.max(lg, axis=-1, keepdims=True)
    i1 = jnp.min(jnp.where(lg == m1, lane, float(LANES)), axis=-1, keepdims=True)
    lg2 = jnp.where(lane == i1, neg, lg)
    m2 = jnp.max(lg2, axis=-1, keepdims=True)
    i2 = jnp.min(jnp.where(lg2 == m2, lane, float(LANES)), axis=-1, keepdims=True)
    e = jnp.exp(m2 - m1)
    w1 = 1.0 / (1.0 + e)
    w2 = e / (1.0 + e)
    idx_ref[...] = jnp.where(lane == 0.0, i1, i2).astype(jnp.int32)
    gw_ref[...] = jnp.where(lane == 0.0, w1, w2)


def _router(x, g, wr, n_experts):
    m, k = x.shape
    bm = _blk(m, 512)
    return pl.pallas_call(
        functools.partial(_router_kernel, n_experts=n_experts),
        out_shape=(jax.ShapeDtypeStruct((m, k), F32),
                   jax.ShapeDtypeStruct((m, LANES), jnp.int32),
                   jax.ShapeDtypeStruct((m, LANES), F32)),
        grid=(m // bm,),
        in_specs=[
            pl.BlockSpec((bm, k), lambda i: (i, 0)),
            pl.BlockSpec((1, k), lambda i: (0, 0)),
            pl.BlockSpec((k, LANES), lambda i: (0, 0)),
        ],
        out_specs=(pl.BlockSpec((bm, k), lambda i: (i, 0)),
                   pl.BlockSpec((bm, LANES), lambda i: (i, 0)),
                   pl.BlockSpec((bm, LANES), lambda i: (i, 0))),
        compiler_params=_cparams("parallel"),
        name="router",
    )(x, g, wr)


def _rank_kernel(e_ref, rank_ref, cnt_ref, carry_ref, *, ep, bl):
    @pl.when(pl.program_id(0) == 0)
    def _():
        carry_ref[...] = jnp.zeros_like(carry_ref)

    e_row = e_ref[0]
    sub = lax.broadcasted_iota(jnp.int32, (ep, bl), 0)
    hit = sub == e_row
    oh = jnp.where(hit, 1.0, 0.0).astype(BF16)
    rr = lax.broadcasted_iota(jnp.int32, (bl, bl), 0)
    cc = lax.broadcasted_iota(jnp.int32, (bl, bl), 1)
    tri = jnp.where(rr <= cc, 1.0, 0.0).astype(BF16)
    pre = _dot(oh, tri)
    carry = carry_ref[...]
    rank = jnp.sum(jnp.where(hit, pre - 1.0 + carry[:, 0:1], 0.0), axis=0, keepdims=True)
    rank_ref[0] = rank.astype(jnp.int32)
    carry = carry + pre[:, bl - 1:bl]
    carry_ref[...] = carry
    cnt_ref[...] = carry.astype(jnp.int32)


def _ranks(e_flat, n_experts):
    n = e_flat.shape[0]
    bl = _blk(n, 512)
    ep = 2 * SUBLANES * (-(-n_experts // (2 * SUBLANES)))
    rank, cnt = pl.pallas_call(
        functools.partial(_rank_kernel, ep=ep, bl=bl),
        out_shape=(jax.ShapeDtypeStruct((n // bl, 1, bl), jnp.int32),
                   jax.ShapeDtypeStruct((ep, LANES), jnp.int32)),
        grid=(n // bl,),
        in_specs=[pl.BlockSpec((1, 1, bl), lambda i: (i, 0, 0))],
        out_specs=(pl.BlockSpec((1, 1, bl), lambda i: (i, 0, 0)),
                   pl.BlockSpec((ep, LANES), lambda i: (0, 0))),
        scratch_shapes=[pltpu.VMEM((ep, LANES), F32)],
        compiler_params=_cparams("arbitrary"),
        name="expert_ranks",
    )(e_flat.reshape(n // bl, 1, bl))
    return rank.reshape(n), cnt[:n_experts, 0]


def _row_copy(src, dst, s_row, d_row, sem):
    return pltpu.make_async_copy(src.at[pl.ds(s_row, 1)], dst.at[pl.ds(d_row, 1)], sem)


def _scatter_kernel(pos_ref, h_ref, xs_in_ref, xs_ref, sem, *, rb, n_tok):
    del xs_in_ref
    base = pl.program_id(0) * rb

    def issue(r, carry):
        t = base + r
        for k in range(TOP_K):
            _row_copy(h_ref, xs_ref, t, pos_ref[k * n_tok + t], sem).start()
        return carry

    lax.fori_loop(0, rb, issue, 0)

    def drain(r, carry):
        for k in range(TOP_K):
            _row_copy(h_ref, xs_ref, 0, 0, sem).wait()
        return carry

    lax.fori_loop(0, rb, drain, 0)


def _scatter_rows(pos, h, n_rows):
    n_tok, k = h.shape
    rb = _blk(n_tok, 256)
    xs0 = jnp.zeros((n_rows, k), h.dtype)
    return pl.pallas_call(
        functools.partial(_scatter_kernel, rb=rb, n_tok=n_tok),
        out_shape=jax.ShapeDtypeStruct((n_rows, k), h.dtype),
        grid_spec=pltpu.PrefetchScalarGridSpec(
            num_scalar_prefetch=1,
            grid=(n_tok // rb,),
            in_specs=[pl.BlockSpec(memory_space=pl.ANY), pl.BlockSpec(memory_space=pl.ANY)],
            out_specs=pl.BlockSpec(memory_space=pl.ANY),
            scratch_shapes=[pltpu.SemaphoreType.DMA],
        ),
        input_output_aliases={2: 0},
        compiler_params=_cparams("arbitrary"),
        name="moe_scatter",
    )(pos, h, xs0)


def _moe_up_kernel(te_ref, nu_ref, xs_ref, wg_ref, wu_ref, o_ref):
    del te_ref

    @pl.when(pl.program_id(1) < nu_ref[0])
    def _():
        h = xs_ref[...].astype(BF16)
        a = _dot(h, wg_ref[...])
        b = _dot(h, wu_ref[...])
        o_ref[...] = (_silu(a) * b).astype(o_ref.dtype)

    @pl.when(pl.program_id(1) >= nu_ref[0])
    def _():
        o_ref[...] = jnp.zeros_like(o_ref)


def _moe_up(te, nu, xs, wg, wu, bm):
    rows, k = xs.shape
    f = wg.shape[2]
    bn = _blk(f, 1024)
    return pl.pallas_call(
        _moe_up_kernel,
        out_shape=jax.ShapeDtypeStruct((rows, f), BF16),
        grid_spec=pltpu.PrefetchScalarGridSpec(
            num_scalar_prefetch=2,
            grid=(f // bn, rows // bm),
            in_specs=[
                pl.BlockSpec((bm, k), lambda j, i, te, nu: (i, 0)),
                pl.BlockSpec((None, k, bn), lambda j, i, te, nu: (te[i], 0, j)),
                pl.BlockSpec((None, k, bn), lambda j, i, te, nu: (te[i], 0, j)),
            ],
            out_specs=pl.BlockSpec((bm, bn), lambda j, i, te, nu: (i, j)),
        ),
        compiler_params=_cparams("parallel", "arbitrary"),
        name="moe_up",
    )(te, nu, xs, wg, wu)


def _moe_down_kernel(te_ref, nu_ref, a_ref, wd_ref, o_ref):
    del te_ref

    @pl.when(pl.program_id(1) < nu_ref[0])
    def _():
        o_ref[...] = _dot(a_ref[...], wd_ref[...])

    @pl.when(pl.program_id(1) >= nu_ref[0])
    def _():
        o_ref[...] = jnp.zeros_like(o_ref)


def _moe_down(te, nu, act, wd, bm):
    rows, f = act.shape
    n = wd.shape[2]
    bn = _blk(n, 512)
    return pl.pallas_call(
        _moe_down_kernel,
        out_shape=jax.ShapeDtypeStruct((rows, n), F32),
        grid_spec=pltpu.PrefetchScalarGridSpec(
            num_scalar_prefetch=2,
            grid=(n // bn, rows // bm),
            in_specs=[
                pl.BlockSpec((bm, f), lambda j, i, te, nu: (i, 0)),
                pl.BlockSpec((None, f, bn), lambda j, i, te, nu: (te[i], 0, j)),
            ],
            out_specs=pl.BlockSpec((bm, bn), lambda j, i, te, nu: (i, j)),
        ),
        compiler_params=_cparams("parallel", "arbitrary"),
        name="moe_down",
    )(te, nu, act, wd)


def _combine_kernel(pos_ref, x_ref, gw_ref, fg_ref, ys_ref, o_ref, buf_ref, sem, *, rb, n_tok):
    base = pl.program_id(0) * rb

    def issue(r, carry):
        t = base + r
        for k in range(TOP_K):
            _row_copy(ys_ref, buf_ref.at[k], pos_ref[k * n_tok + t], r, sem).start()
        return carry

    lax.fori_loop(0, rb, issue, 0)

    def drain(r, carry):
        for k in range(TOP_K):
            _row_copy(ys_ref, buf_ref.at[k], 0, 0, sem).wait()
        return carry

    lax.fori_loop(0, rb, drain, 0)

    gw = gw_ref[...]
    y = x_ref[...] + gw[:, 0:1] * buf_ref[0] + gw[:, 1:2] * buf_ref[1]
    o_ref[...] = _rms(y, fg_ref[...])


def _combine_norm(pos, x, gw, fg, ys):
    n_tok, d = x.shape
    rb = _blk(n_tok, 256)
    return pl.pallas_call(
        functools.partial(_combine_kernel, rb=rb, n_tok=n_tok),
        out_shape=jax.ShapeDtypeStruct((n_tok, d), F32),
        grid_spec=pltpu.PrefetchScalarGridSpec(
            num_scalar_prefetch=1,
            grid=(n_tok // rb,),
            in_specs=[
                pl.BlockSpec((rb, d), lambda i, pos: (i, 0)),
                pl.BlockSpec((rb, LANES), lambda i, pos: (i, 0)),
                pl.BlockSpec((1, d), lambda i, pos: (0, 0)),
                pl.BlockSpec(memory_space=pl.ANY),
            ],
            out_specs=pl.BlockSpec((rb, d), lambda i, pos: (i, 0)),
            scratch_shapes=[pltpu.VMEM((TOP_K, rb, d), F32), pltpu.SemaphoreType.DMA],
        ),
        compiler_params=_cparams("arbitrary"),
        name="moe_combine_norm",
    )(pos, x, gw, fg, ys)


def _moe_block(x, norm_g, final_g, w_router, w_gate, w_up, w_down):
    n_tok, d = x.shape
    n_experts = w_router.shape[1]
    bm = _blk(n_tok, 512)
    wr = jnp.zeros((d, LANES), F32).at[:, :n_experts].set(w_router)
    h, idx, gw = _router(x, norm_g, wr, n_experts)

    e_flat = idx[:, :TOP_K].T.reshape(TOP_K * n_tok)
    rank, counts = _ranks(e_flat, n_experts)

    tiles = (counts + bm - 1) // bm
    tile_end = jnp.cumsum(tiles)
    row_off = (tile_end - tiles) * bm
    pos = row_off[e_flat] + rank
    n_tiles = TOP_K * n_tok // bm + n_experts
    n_used = tile_end[-1]
    tile_ids = jnp.minimum(jnp.arange(n_tiles, dtype=jnp.int32), n_used - 1)
    te = jnp.sum(tile_ids[:, None] >= tile_end[None, :], axis=1).astype(jnp.int32)
    nu = n_used.reshape(1).astype(jnp.int32)

    xs = _scatter_rows(pos, h, n_tiles * bm)
    act = _moe_up(te, nu, xs, w_gate, w_up, bm)
    ys = _moe_down(te, nu, act, w_down, bm)
    return _combine_norm(pos, x, gw, final_g, ys)


def _conformer_layer(x, bsz, seq, mix_g, ffn_g, pw1_w, pw1_b, dw_w, dw_b, ln_g, ln_b,
                     pw2_w, pw2_b, w_gate, w_up, w_down):
    d = x.shape[1]
    row = lambda v: v.reshape(1, -1)
    glu = _norm_glu(x, row(mix_g), pw1_w.astype(BF16), row(pw1_b), F32)
    hc = _dwconv(glu.reshape(bsz, seq, d), dw_w, row(dw_b), (row(ln_g), row(ln_b)), BF16)
    x = _mm_res(hc.reshape(bsz * seq, d), pw2_w.astype(BF16), row(pw2_b), x)
    act = _norm_swiglu(x, row(ffn_g), w_gate.astype(BF16), w_up.astype(BF16))
    return _mm_res(act, w_down.astype(BF16), jnp.zeros((1, d), F32), x)


def _mamba_mixer(x, bsz, seq, mix_g, w_in, conv_w, conv_b, dt_bias, a_log, d_skip, norm_g, w_out):
    d = x.shape[1]
    heads = dt_bias.shape[0]
    d_inner = norm_g.shape[0]
    conv_dim = conv_w.shape[1]
    n_state = (conv_dim - d_inner) // (2 * SSM_GROUPS)
    hd = d_inner // heads
    row = lambda v: v.reshape(1, -1)
    n_main = d_inner + conv_dim
    w_main = w_in[:, :n_main].astype(BF16)
    w_dt = jnp.zeros((d, LANES), BF16).at[:, :heads].set(w_in[:, n_main:].astype(BF16))
    zx, dt_raw = _norm_inproj(x, row(mix_g), w_main, w_dt)
    xbc = _dwconv(zx.reshape(bsz, seq, n_main), conv_w, row(conv_b), None, F32, c_start=d_inner)
    y = _ssd(zx, xbc.reshape(bsz * seq, conv_dim), dt_raw[:, :heads].T,
             dt_bias.reshape(heads, 1), a_log.reshape(heads, 1),
             row(jnp.repeat(d_skip, hd)), row(norm_g),
             bsz=bsz, seq=seq, d_inner=d_inner, n_state=n_state)
    return _mm_res(y, w_out.astype(BF16), jnp.zeros((1, d), F32), x)


def kernel(x, norm_mix_g, norm_ffn_g, final_norm_g, conf_pw1_w, conf_pw1_b, conf_dw_w, conf_dw_b, conf_ln_g, conf_ln_b, conf_pw2_w, conf_pw2_b, ssm_in_w, ssm_conv_w, ssm_conv_b, ssm_dt_bias, ssm_a_log, ssm_d, ssm_norm_g, ssm_out_w, ffn_w_gate, ffn_w_up, ffn_w_down, moe_router_w, moe_w_gate, moe_w_up, moe_w_down):
    bsz, seq, d = x.shape
    assert norm_mix_g.shape[0] == 2, "two layers: Conformer conv + SwiGLU, then Mamba-2 + MoE"
    xf = x.reshape(bsz * seq, d)
    xf = _conformer_layer(xf, bsz, seq, norm_mix_g[0], norm_ffn_g[0], conf_pw1_w[0], conf_pw1_b[0],
                          conf_dw_w[0], conf_dw_b[0], conf_ln_g[0], conf_ln_b[0], conf_pw2_w[0],
                          conf_pw2_b[0], ffn_w_gate[0], ffn_w_up[0], ffn_w_down[0])
    xf = _mamba_mixer(xf, bsz, seq, norm_mix_g[1], ssm_in_w[0], ssm_conv_w[0], ssm_conv_b[0],
                      ssm_dt_bias[0], ssm_a_log[0], ssm_d[0], ssm_norm_g[0], ssm_out_w[0])
    out = _moe_block(xf, norm_ffn_g[1].reshape(1, d), final_norm_g.reshape(1, d),
                     moe_router_w[0], moe_w_gate[0].astype(BF16), moe_w_up[0].astype(BF16),
                     moe_w_down[0].astype(BF16))
    return out.reshape(bsz, seq, d)
```

```python
import functools

import jax
import jax.numpy as jnp
from jax import lax
from jax.experimental import pallas as pl
from jax.experimental.pallas import tpu as pltpu

EPS = 1e-6
SSM_GROUPS = 8
SSM_CHUNK = 256
TOP_K = 2
LANES = 128
SUBLANES = 8
VMEM_LIMIT = 56 << 20

F32 = jnp.float32
BF16 = jnp.bfloat16


def _cparams(*sem):
    return pltpu.CompilerParams(dimension_semantics=sem, vmem_limit_bytes=VMEM_LIMIT)


def _sigmoid(v):
    return 1.0 / (1.0 + jnp.exp(-v))


def _silu(v):
    return v * _sigmoid(v)


def _rms(x, g):
    return x * lax.rsqrt(jnp.mean(x * x, axis=-1, keepdims=True) + EPS) * g


def _dot(a, b):
    return jnp.dot(a, b, preferred_element_type=F32)


def _dot_nt(a, b):
    return lax.dot_general(a, b, (((1,), (1,)), ((), ())), preferred_element_type=F32)


def _blk(n, want):
    b = min(n, want)
    assert n % b == 0, (n, want)
    return b


def _norm_glu_kernel(x_ref, g_ref, wa_ref, wb_ref, ba_ref, bb_ref, o_ref, h_ref):
    @pl.when(pl.program_id(1) == 0)
    def _():
        h_ref[...] = _rms(x_ref[...], g_ref[...]).astype(BF16)

    h = h_ref[...]
    a = _dot(h, wa_ref[...]) + ba_ref[...]
    b = _dot(h, wb_ref[...]) + bb_ref[...]
    o_ref[...] = (a * _sigmoid(b)).astype(o_ref.dtype)


def _norm_glu(x, g, w, b, out_dtype):
    m, k = x.shape
    n = w.shape[1] // 2
    bm, bn = _blk(m, 1024), _blk(n, 512)
    nb = n // bn
    return pl.pallas_call(
        _norm_glu_kernel,
        out_shape=jax.ShapeDtypeStruct((m, n), out_dtype),
        grid=(m // bm, nb),
        in_specs=[
            pl.BlockSpec((bm, k), lambda i, j: (i, 0)),
            pl.BlockSpec((1, k), lambda i, j: (0, 0)),
            pl.BlockSpec((k, bn), lambda i, j: (0, j)),
            pl.BlockSpec((k, bn), lambda i, j: (0, j + nb)),
            pl.BlockSpec((1, bn), lambda i, j: (0, j)),
            pl.BlockSpec((1, bn), lambda i, j: (0, j + nb)),
        ],
        out_specs=pl.BlockSpec((bm, bn), lambda i, j: (i, j)),
        scratch_shapes=[pltpu.VMEM((bm, k), BF16)],
        compiler_params=_cparams("parallel", "arbitrary"),
        name="norm_glu",
    )(x, g, w, w, b, b)


def _norm_swiglu_kernel(x_ref, g_ref, wg_ref, wu_ref, o_ref, h_ref):
    @pl.when(pl.program_id(1) == 0)
    def _():
        h_ref[...] = _rms(x_ref[...], g_ref[...]).astype(BF16)

    h = h_ref[...]
    a = _dot(h, wg_ref[...])
    b = _dot(h, wu_ref[...])
    o_ref[...] = (_silu(a) * b).astype(o_ref.dtype)


def _norm_swiglu(x, g, wg, wu):
    m, k = x.shape
    n = wg.shape[1]
    bm, bn = _blk(m, 1024), _blk(n, 512)
    return pl.pallas_call(
        _norm_swiglu_kernel,
        out_shape=jax.ShapeDtypeStruct((m, n), BF16),
        grid=(m // bm, n // bn),
        in_specs=[
            pl.BlockSpec((bm, k), lambda i, j: (i, 0)),
            pl.BlockSpec((1, k), lambda i, j: (0, 0)),
            pl.BlockSpec((k, bn), lambda i, j: (0, j)),
            pl.BlockSpec((k, bn), lambda i, j: (0, j)),
        ],
        out_specs=pl.BlockSpec((bm, bn), lambda i, j: (i, j)),
        scratch_shapes=[pltpu.VMEM((bm, k), BF16)],
        compiler_params=_cparams("parallel", "arbitrary"),
        name="norm_swiglu",
    )(x, g, wg, wu)


def _norm_inproj_kernel(x_ref, g_ref, w_ref, wdt_ref, o_ref, dt_ref, h_ref):
    @pl.when(pl.program_id(1) == 0)
    def _():
        h = _rms(x_ref[...], g_ref[...]).astype(BF16)
        h_ref[...] = h
        dt_ref[...] = _dot(h, wdt_ref[...])

    o_ref[...] = _dot(h_ref[...], w_ref[...])


def _norm_inproj(x, g, w, wdt):
    m, k = x.shape
    n = w.shape[1]
    bm, bn = _blk(m, 1024), _blk(n, 512)
    return pl.pallas_call(
        _norm_inproj_kernel,
        out_shape=(jax.ShapeDtypeStruct((m, n), F32),
                   jax.ShapeDtypeStruct((m, wdt.shape[1]), F32)),
        grid=(m // bm, n // bn),
        in_specs=[
            pl.BlockSpec((bm, k), lambda i, j: (i, 0)),
            pl.BlockSpec((1, k), lambda i, j: (0, 0)),
            pl.BlockSpec((k, bn), lambda i, j: (0, j)),
            pl.BlockSpec((k, wdt.shape[1]), lambda i, j: (0, 0)),
        ],
        out_specs=(pl.BlockSpec((bm, bn), lambda i, j: (i, j)),
                   pl.BlockSpec((bm, wdt.shape[1]), lambda i, j: (i, 0))),
        scratch_shapes=[pltpu.VMEM((bm, k), BF16)],
        compiler_params=_cparams("parallel", "arbitrary"),
        name="norm_inproj",
    )(x, g, w, wdt)


def _mm_res_kernel(a_ref, w_ref, b_ref, r_ref, o_ref):
    o_ref[...] = r_ref[...] + _dot(a_ref[...], w_ref[...]) + b_ref[...]


def _mm_res(a, w, bias, res):
    m, k = a.shape
    n = w.shape[1]
    bm = _blk(m, 1024)
    bn = _blk(n, 256 if k > 4096 else 512)
    return pl.pallas_call(
        _mm_res_kernel,
        out_shape=jax.ShapeDtypeStruct((m, n), F32),
        grid=(m // bm, n // bn),
        in_specs=[
            pl.BlockSpec((bm, k), lambda i, j: (i, 0)),
            pl.BlockSpec((k, bn), lambda i, j: (0, j)),
            pl.BlockSpec((1, bn), lambda i, j: (0, j)),
            pl.BlockSpec((bm, bn), lambda i, j: (i, j)),
        ],
        out_specs=pl.BlockSpec((bm, bn), lambda i, j: (i, j)),
        compiler_params=_cparams("parallel", "arbitrary"),
        name="mm_res",
    )(a, w, bias, res)


CONV_ROWS = 32


def _dwconv_kernel(*refs, taps, halo, bt, cb, layer_norm):
    if layer_norm:
        xc_ref, xh_ref, w_ref, b_ref, lg_ref, lb_ref, o_ref, win_ref, y_ref = refs
    else:
        xc_ref, xh_ref, w_ref, b_ref, o_ref, win_ref = refs
        y_ref = None
    first = pl.program_id(1) == 0
    win_ref[0, 0:halo, :] = jnp.where(first, 0.0, xh_ref[...])
    win_ref[0, halo:halo + bt, :] = xc_ref[...]
    off = halo - (taps - 1)
    n_sh = halo + bt - SUBLANES
    for s in sorted({(off + j) % SUBLANES for j in range(taps)} - {0}):
        win_ref[s, 0:n_sh, :] = win_ref[0, s:s + n_sh, :]

    def body(r, carry):
        r0 = pl.multiple_of(r * CONV_ROWS, CONV_ROWS)
        for c in range(cb // LANES):
            cs = slice(c * LANES, (c + 1) * LANES)
            acc = jnp.broadcast_to(b_ref[:, cs], (CONV_ROWS, LANES))
            for j in range(taps):
                al, s = divmod(off + j, SUBLANES)
                xw = win_ref[s, pl.ds(r0 + al * SUBLANES, CONV_ROWS), cs]
                acc = acc + xw * w_ref[j:j + 1, cs]
            if layer_norm:
                y_ref[pl.ds(r0, CONV_ROWS), cs] = acc
            else:
                o_ref[pl.ds(r0, CONV_ROWS), cs] = _silu(acc).astype(o_ref.dtype)
        return carry

    lax.fori_loop(0, bt // CONV_ROWS, body, 0)

    if layer_norm:
        y = y_ref[...]
        mu = jnp.mean(y, axis=-1, keepdims=True)
        d = y - mu
        var = jnp.mean(d * d, axis=-1, keepdims=True)
        hn = d * lax.rsqrt(var + EPS) * lg_ref[...] + lb_ref[...]
        o_ref[...] = _silu(hn).astype(o_ref.dtype)


def _dwconv(x3, w, b, ln, out_dtype, c_start=0):
    bsz, seq, _ = x3.shape
    taps, ch = w.shape
    halo = SUBLANES * (-(-(taps - 1) // SUBLANES))
    bt = _blk(seq, 256)
    assert bt % halo == 0 and bt % CONV_ROWS == 0
    layer_norm = ln is not None
    cb = ch if layer_norm else _blk(ch, 1024)
    assert c_start % cb == 0
    co = c_start // cb
    hb = bt // halo
    in_specs = [
        pl.BlockSpec((None, bt, cb), lambda bi, i, c: (bi, i, c + co)),
        pl.BlockSpec((None, halo, cb),
                     lambda bi, i, c: (bi, jnp.maximum(i * hb - 1, 0), c + co)),
        pl.BlockSpec((taps, cb), lambda bi, i, c: (0, c)),
        pl.BlockSpec((1, cb), lambda bi, i, c: (0, c)),
    ]
    args = [x3, x3, w, b]
    scratch = [pltpu.VMEM((SUBLANES, halo + bt, cb), F32)]
    if layer_norm:
        in_specs += [pl.BlockSpec((1, cb), lambda bi, i, c: (0, c))] * 2
        args += [ln[0], ln[1]]
        scratch.append(pltpu.VMEM((bt, cb), F32))
    return pl.pallas_call(
        functools.partial(_dwconv_kernel, taps=taps, halo=halo, bt=bt, cb=cb,
                          layer_norm=layer_norm),
        out_shape=jax.ShapeDtypeStruct((bsz, seq, ch), out_dtype),
        grid=(bsz, seq // bt, ch // cb),
        in_specs=in_specs,
        out_specs=pl.BlockSpec((None, bt, cb), lambda bi, i, c: (bi, i, c)),
        scratch_shapes=scratch,
        compiler_params=_cparams("parallel", "parallel", "parallel"),
        name="dwconv_ln" if layer_norm else "dwconv",
    )(*args)


def _split3(v):
    hi = v.astype(BF16)
    r = v - hi.astype(F32)
    mid = r.astype(BF16)
    lo = (r - mid.astype(F32)).astype(BF16)
    return hi, mid, lo


def _expand_heads(v, hpg, hd):
    rows = v.shape[0]
    lane_head = lax.broadcasted_iota(jnp.int32, (rows, hpg * hd), 1) // hd
    out = jnp.zeros((rows, hpg * hd), F32)
    for h in range(hpg):
        out = jnp.where(lane_head == h, v[:, h:h + 1], out)
    return out


def _ssd_kernel(x_ref, b_ref, c_ref, z_ref, dtt_ref, dtb_ref, alog_ref, dexp_ref, ng_ref,
                o_ref, s_ref, y_ref, *, hpg, hd, q):
    @pl.when(pl.program_id(2) == 0)
    def _():
        s_ref[...] = jnp.zeros_like(s_ref)

    xg = x_ref[...]
    bg = b_ref[...]
    cg = c_ref[...]
    cgb = cg.astype(BF16)
    v = dtt_ref[...] + dtb_ref[...]
    dt = jnp.maximum(v, 0.0) + jnp.log1p(jnp.exp(-jnp.abs(v)))
    a = dt * (-jnp.exp(alog_ref[...]))

    a_pad = jnp.concatenate([a, jnp.zeros((LANES - hpg, q), F32)], axis=0)
    rr = lax.broadcasted_iota(jnp.int32, (q, q), 0)
    cc = lax.broadcasted_iota(jnp.int32, (q, q), 1)
    causal = rr >= cc
    tri_ls = jnp.where(causal, 1.0, 0.0).astype(BF16)
    tri_sl = jnp.where(rr <= cc, 1.0, 0.0).astype(BF16)
    acs_t = jnp.zeros((LANES, q), F32)
    acs = jnp.zeros((q, LANES), F32)
    for piece in _split3(a_pad):
        acs_t = acs_t + _dot(piece, tri_sl)
        acs = acs + _dot_nt(tri_ls, piece)
    acs_t = acs_t[0:hpg, :]
    wst_t = jnp.exp(acs_t[:, q - 1:q] - acs_t) * dt

    cb = _dot_nt(cgb, bg.astype(BF16))
    bt = bg.T
    lane = lax.broadcasted_iota(jnp.int32, (q, LANES), 1)
    pair = LANES // hd
    s_old = s_ref[...]
    for t in range(hpg // pair):
        ts = slice(t * LANES, (t + 1) * LANES)
        x_t = xg[:, ts]
        y_t = jnp.zeros((q, LANES), F32)
        s_t = jnp.zeros((bt.shape[0], LANES), F32)
        for p in range(pair):
            h = t * pair + p
            seg = acs[:, h:h + 1] - acs_t[h:h + 1, :]
            decay = jnp.exp(jnp.where(causal, seg, -jnp.inf))
            m = (cb * decay * dt[h:h + 1, :]).astype(BF16)
            xm = jnp.where((lane >= p * hd) & (lane < (p + 1) * hd), x_t, 0.0).astype(BF16)
            y_t = y_t + _dot(m, xm)
            bw = (bt * wst_t[h:h + 1, :]).astype(BF16)
            s_t = s_t + _dot(bw, xm)
        y_ref[:, ts] = y_t
        dec_t = _expand_heads(jnp.exp(acs[q - 1:q, t * pair:(t + 1) * pair]), pair, hd)
        s_ref[:, ts] = s_old[:, ts] * dec_t + s_t

    y = y_ref[...]
    y = y + _dot(cgb, s_old.astype(BF16)) * _expand_heads(jnp.exp(acs[:, 0:hpg]), hpg, hd)
    y = y + xg * dexp_ref[...]
    y = y * _silu(z_ref[...])
    y = y * lax.rsqrt(jnp.mean(y * y, axis=-1, keepdims=True) + EPS) * ng_ref[...]
    o_ref[...] = y.astype(o_ref.dtype)


def _ssd(zx, xbc, dtt, dt_bias, a_log, d_exp, norm_g, *, bsz, seq, d_inner, n_state):
    groups = SSM_GROUPS
    heads = dtt.shape[0]
    hpg = heads // groups
    hd = d_inner // heads
    gw = hpg * hd
    q = _blk(seq, SSM_CHUNK)
    nc = seq // q
    assert gw % LANES == 0 and LANES % hd == 0 and n_state % LANES == 0
    assert d_inner // groups == gw
    b_off = d_inner // n_state
    c_off = b_off + groups
    row = lambda b, g, c: b * nc + c
    return pl.pallas_call(
        functools.partial(_ssd_kernel, hpg=hpg, hd=hd, q=q),
        out_shape=jax.ShapeDtypeStruct((bsz * seq, d_inner), BF16),
        grid=(bsz, groups, nc),
        in_specs=[
            pl.BlockSpec((q, gw), lambda b, g, c: (row(b, g, c), g)),
            pl.BlockSpec((q, n_state), lambda b, g, c: (row(b, g, c), b_off + g)),
            pl.BlockSpec((q, n_state), lambda b, g, c: (row(b, g, c), c_off + g)),
            pl.BlockSpec((q, gw), lambda b, g, c: (row(b, g, c), g)),
            pl.BlockSpec((hpg, q), lambda b, g, c: (g, row(b, g, c))),
            pl.BlockSpec((hpg, 1), lambda b, g, c: (g, 0)),
            pl.BlockSpec((hpg, 1), lambda b, g, c: (g, 0)),
            pl.BlockSpec((1, gw), lambda b, g, c: (0, g)),
            pl.BlockSpec((1, gw), lambda b, g, c: (0, g)),
        ],
        out_specs=pl.BlockSpec((q, gw), lambda b, g, c: (row(b, g, c), g)),
        scratch_shapes=[pltpu.VMEM((n_state, gw), F32), pltpu.VMEM((q, gw), F32)],
        compiler_params=_cparams("parallel", "parallel", "arbitrary"),
        name="ssd",
    )(xbc, xbc, xbc, zx, dtt, dt_bias, a_log, d_exp, norm_g)


def _router_kernel(x_ref, g_ref, wr_ref, h_ref, idx_ref, gw_ref, *, n_experts):
    h = _rms(x_ref[...], g_ref[...])
    h_ref[...] = h
    logits = jnp.dot(h, wr_ref[...], preferred_element_type=F32,
                     precision=lax.Precision.HIGHEST)
    lane = lax.broadcasted_iota(jnp.int32, logits.shape, 1).astype(F32)
    neg = -jnp.inf
    lg = jnp.where(lane < n_experts, logits, neg)
    m1 = jnp.max(lg, axis=-1, keepdims=True)
    i1 = jnp.min(jnp.where(lg == m1, lane, float(LANES)), axis=-1, keepdims=True)
    lg2 = jnp.where(lane == i1, neg, lg)
    m2 = jnp.max(lg2, axis=-1, keepdims=True)
    i2 = jnp.min(jnp.where(lg2 == m2, lane, float(LANES)), axis=-1, keepdims=True)
    e = jnp.exp(m2 - m1)
    w1 = 1.0 / (1.0 + e)
    w2 = e / (1.0 + e)
    idx_ref[...] = jnp.where(lane == 0.0, i1, i2).astype(jnp.int32)
    gw_ref[...] = jnp.where(lane == 0.0, w1, w2)


def _router(x, g, wr, n_experts):
    m, k = x.shape
    bm = _blk(m, 512)
    return pl.pallas_call(
        functools.partial(_router_kernel, n_experts=n_experts),
        out_shape=(jax.ShapeDtypeStruct((m, k), F32),
                   jax.ShapeDtypeStruct((m, LANES), jnp.int32),
                   jax.ShapeDtypeStruct((m, LANES), F32)),
        grid=(m // bm,),
        in_specs=[
            pl.BlockSpec((bm, k), lambda i: (i, 0)),
            pl.BlockSpec((1, k), lambda i: (0, 0)),
            pl.BlockSpec((k, LANES), lambda i: (0, 0)),
        ],
        out_specs=(pl.BlockSpec((bm, k), lambda i: (i, 0)),
                   pl.BlockSpec((bm, LANES), lambda i: (i, 0)),
                   pl.BlockSpec((bm, LANES), lambda i: (i, 0))),
        compiler_params=_cparams("parallel"),
        name="router",
    )(x, g, wr)


def _rank_kernel(e_ref, rank_ref, cnt_ref, carry_ref, *, ep, bl):
    @pl.when(pl.program_id(0) == 0)
    def _():
        carry_ref[...] = jnp.zeros_like(carry_ref)

    e_row = e_ref[0]
    sub = lax.broadcasted_iota(jnp.int32, (ep, bl), 0)
    hit = sub == e_row
    oh = jnp.where(hit, 1.0, 0.0).astype(BF16)
    rr = lax.broadcasted_iota(jnp.int32, (bl, bl), 0)
    cc = lax.broadcasted_iota(jnp.int32, (bl, bl), 1)
    tri = jnp.where(rr <= cc, 1.0, 0.0).astype(BF16)
    pre = _dot(oh, tri)
    carry = carry_ref[...]
    rank = jnp.sum(jnp.where(hit, pre - 1.0 + carry[:, 0:1], 0.0), axis=0, keepdims=True)
    rank_ref[0] = rank.astype(jnp.int32)
    carry = carry + pre[:, bl - 1:bl]
    carry_ref[...] = carry
    cnt_ref[...] = carry.astype(jnp.int32)


def _ranks(e_flat, n_experts):
    n = e_flat.shape[0]
    bl = _blk(n, 512)
    ep = 2 * SUBLANES * (-(-n_experts // (2 * SUBLANES)))
    rank, cnt = pl.pallas_call(
        functools.partial(_rank_kernel, ep=ep, bl=bl),
        out_shape=(jax.ShapeDtypeStruct((n // bl, 1, bl), jnp.int32),
                   jax.ShapeDtypeStruct((ep, LANES), jnp.int32)),
        grid=(n // bl,),
        in_specs=[pl.BlockSpec((1, 1, bl), lambda i: (i, 0, 0))],
        out_specs=(pl.BlockSpec((1, 1, bl), lambda i: (i, 0, 0)),
                   pl.BlockSpec((ep, LANES), lambda i: (0, 0))),
        scratch_shapes=[pltpu.VMEM((ep, LANES), F32)],
        compiler_params=_cparams("arbitrary"),
        name="expert_ranks",
    )(e_flat.reshape(n // bl, 1, bl))
    return rank.reshape(n), cnt[:n_experts, 0]


def _row_copy(src, dst, s_row, d_row, sem):
    return pltpu.make_async_copy(src.at[pl.ds(s_row, 1)], dst.at[pl.ds(d_row, 1)], sem)


def _scatter_kernel(pos_ref, h_ref, xs_in_ref, xs_ref, sem, *, rb, n_tok):
    del xs_in_ref
    base = pl.program_id(0) * rb

    def issue(r, carry):
        for k in range(TOP_K):
            _row_copy(h_ref, xs_ref, r, pos_ref[k * n_tok + base + r], sem).start()
        return carry

    lax.fori_loop(0, rb, issue, 0, unroll=8)

    def drain(r, carry):
        for k in range(TOP_K):
            _row_copy(h_ref, xs_ref, 0, 0, sem).wait()
        return carry

    lax.fori_loop(0, rb, drain, 0, unroll=8)


def _scatter_rows(pos, h, n_rows):
    n_tok, k = h.shape
    rb = _blk(n_tok, 256)
    xs0 = jnp.zeros((n_rows, k), h.dtype)
    return pl.pallas_call(
        functools.partial(_scatter_kernel, rb=rb, n_tok=n_tok),
        out_shape=jax.ShapeDtypeStruct((n_rows, k), h.dtype),
        grid_spec=pltpu.PrefetchScalarGridSpec(
            num_scalar_prefetch=1,
            grid=(n_tok // rb,),
            in_specs=[pl.BlockSpec((rb, k), lambda i, pos: (i, 0)),
                      pl.BlockSpec(memory_space=pl.ANY)],
            out_specs=pl.BlockSpec(memory_space=pl.ANY),
            scratch_shapes=[pltpu.SemaphoreType.DMA],
        ),
        input_output_aliases={2: 0},
        compiler_params=_cparams("arbitrary"),
        name="moe_scatter",
    )(pos, h, xs0)


def _moe_up_kernel(te_ref, nu_ref, xs_ref, wg_ref, wu_ref, o_ref):
    del te_ref

    @pl.when(pl.program_id(1) < nu_ref[0])
    def _():
        h = xs_ref[...].astype(BF16)
        a = _dot(h, wg_ref[...])
        b = _dot(h, wu_ref[...])
        o_ref[...] = (_silu(a) * b).astype(o_ref.dtype)

    @pl.when(pl.program_id(1) >= nu_ref[0])
    def _():
        o_ref[...] = jnp.zeros_like(o_ref)


def _moe_up(te, nu, xs, wg, wu, bm):
    rows, k = xs.shape
    f = wg.shape[2]
    bn = _blk(f, 1024)
    return pl.pallas_call(
        _moe_up_kernel,
        out_shape=jax.ShapeDtypeStruct((rows, f), BF16),
        grid_spec=pltpu.PrefetchScalarGridSpec(
            num_scalar_prefetch=2,
            grid=(f // bn, rows // bm),
            in_specs=[
                pl.BlockSpec((bm, k), lambda j, i, te, nu: (i, 0)),
                pl.BlockSpec((None, k, bn), lambda j, i, te, nu: (te[i], 0, j)),
                pl.BlockSpec((None, k, bn), lambda j, i, te, nu: (te[i], 0, j)),
            ],
            out_specs=pl.BlockSpec((bm, bn), lambda j, i, te, nu: (i, j)),
        ),
        compiler_params=_cparams("parallel", "arbitrary"),
        name="moe_up",
    )(te, nu, xs, wg, wu)


def _moe_down_kernel(te_ref, nu_ref, a_ref, wd_ref, o_ref):
    del te_ref

    @pl.when(pl.program_id(1) < nu_ref[0])
    def _():
        o_ref[...] = _dot(a_ref[...], wd_ref[...])

    @pl.when(pl.program_id(1) >= nu_ref[0])
    def _():
        o_ref[...] = jnp.zeros_like(o_ref)


def _moe_down(te, nu, act, wd, bm):
    rows, f = act.shape
    n = wd.shape[2]
    bn = _blk(n, 512)
    return pl.pallas_call(
        _moe_down_kernel,
        out_shape=jax.ShapeDtypeStruct((rows, n), F32),
        grid_spec=pltpu.PrefetchScalarGridSpec(
            num_scalar_prefetch=2,
            grid=(n // bn, rows // bm),
            in_specs=[
                pl.BlockSpec((bm, f), lambda j, i, te, nu: (i, 0)),
                pl.BlockSpec((None, f, bn), lambda j, i, te, nu: (te[i], 0, j)),
            ],
            out_specs=pl.BlockSpec((bm, bn), lambda j, i, te, nu: (i, j)),
        ),
        compiler_params=_cparams("parallel", "arbitrary"),
        name="moe_down",
    )(te, nu, act, wd)


def _combine_kernel(pos_ref, x_ref, gw_ref, fg_ref, ys_ref, o_ref, buf_ref, sem, *, rb, n_tok):
    base = pl.program_id(0) * rb

    def issue(r, carry):
        t = base + r
        for k in range(TOP_K):
            _row_copy(ys_ref, buf_ref.at[k], pos_ref[k * n_tok + t], r, sem).start()
        return carry

    lax.fori_loop(0, rb, issue, 0, unroll=8)

    def drain(r, carry):
        for k in range(TOP_K):
            _row_copy(ys_ref, buf_ref.at[k], 0, 0, sem).wait()
        return carry

    lax.fori_loop(0, rb, drain, 0, unroll=8)

    gw = gw_ref[...]
    y = x_ref[...] + gw[:, 0:1] * buf_ref[0] + gw[:, 1:2] * buf_ref[1]
    o_ref[...] = _rms(y, fg_ref[...])


def _combine_norm(pos, x, gw, fg, ys):
    n_tok, d = x.shape
    rb = _blk(n_tok, 256)
    return pl.pallas_call(
        functools.partial(_combine_kernel, rb=rb, n_tok=n_tok),
        out_shape=jax.ShapeDtypeStruct((n_tok, d), F32),
        grid_spec=pltpu.PrefetchScalarGridSpec(
            num_scalar_prefetch=1,
            grid=(n_tok // rb,),
            in_specs=[
                pl.BlockSpec((rb, d), lambda i, pos: (i, 0)),
                pl.BlockSpec((rb, LANES), lambda i, pos: (i, 0)),
                pl.BlockSpec((1, d), lambda i, pos: (0, 0)),
                pl.BlockSpec(memory_space=pl.ANY),
            ],
            out_specs=pl.BlockSpec((rb, d), lambda i, pos: (i, 0)),
            scratch_shapes=[pltpu.VMEM((TOP_K, rb, d), F32), pltpu.SemaphoreType.DMA],
        ),
        compiler_params=_cparams("arbitrary"),
        name="moe_combine_norm",
    )(pos, x, gw, fg, ys)


def _moe_block(x, norm_g, final_g, w_router, w_gate, w_up, w_down):
    n_tok, d = x.shape
    n_experts = w_router.shape[1]
    bm = _blk(n_tok, 512)
    wr = jnp.zeros((d, LANES), F32).at[:, :n_experts].set(w_router)
    h, idx, gw = _router(x, norm_g, wr, n_experts)

    e_flat = idx[:, :TOP_K].T.reshape(TOP_K * n_tok)
    rank, counts = _ranks(e_flat, n_experts)

    tiles = (counts + bm - 1) // bm
    tile_end = jnp.cumsum(tiles)
    row_off = (tile_end - tiles) * bm
    pos = row_off[e_flat] + rank
    n_tiles = TOP_K * n_tok // bm + n_experts
    n_used = tile_end[-1]
    tile_ids = jnp.minimum(jnp.arange(n_tiles, dtype=jnp.int32), n_used - 1)
    te = jnp.sum(tile_ids[:, None] >= tile_end[None, :], axis=1).astype(jnp.int32)
    nu = n_used.reshape(1).astype(jnp.int32)

    xs = _scatter_rows(pos, h, n_tiles * bm)
    act = _moe_up(te, nu, xs, w_gate, w_up, bm)
    ys = _moe_down(te, nu, act, w_down, bm)
    return _combine_norm(pos, x, gw, final_g, ys)


def _conformer_layer(x, bsz, seq, mix_g, ffn_g, pw1_w, pw1_b, dw_w, dw_b, ln_g, ln_b,
                     pw2_w, pw2_b, w_gate, w_up, w_down):
    d = x.shape[1]
    row = lambda v: v.reshape(1, -1)
    glu = _norm_glu(x, row(mix_g), pw1_w.astype(BF16), row(pw1_b), F32)
    hc = _dwconv(glu.reshape(bsz, seq, d), dw_w, row(dw_b), (row(ln_g), row(ln_b)), BF16)
    x = _mm_res(hc.reshape(bsz * seq, d), pw2_w.astype(BF16), row(pw2_b), x)
    act = _norm_swiglu(x, row(ffn_g), w_gate.astype(BF16), w_up.astype(BF16))
    return _mm_res(act, w_down.astype(BF16), jnp.zeros((1, d), F32), x)


def _mamba_mixer(x, bsz, seq, mix_g, w_in, conv_w, conv_b, dt_bias, a_log, d_skip, norm_g, w_out):
    d = x.shape[1]
    heads = dt_bias.shape[0]
    d_inner = norm_g.shape[0]
    conv_dim = conv_w.shape[1]
    n_state = (conv_dim - d_inner) // (2 * SSM_GROUPS)
    hd = d_inner // heads
    row = lambda v: v.reshape(1, -1)
    n_main = d_inner + conv_dim
    w_main = w_in[:, :n_main].astype(BF16)
    w_dt = jnp.zeros((d, LANES), BF16).at[:, :heads].set(w_in[:, n_main:].astype(BF16))
    zx, dt_raw = _norm_inproj(x, row(mix_g), w_main, w_dt)
    xbc = _dwconv(zx.reshape(bsz, seq, n_main), conv_w, row(conv_b), None, F32, c_start=d_inner)
    y = _ssd(zx, xbc.reshape(bsz * seq, conv_dim), dt_raw[:, :heads].T,
             dt_bias.reshape(heads, 1), a_log.reshape(heads, 1),
             row(jnp.repeat(d_skip, hd)), row(norm_g),
             bsz=bsz, seq=seq, d_inner=d_inner, n_state=n_state)
    return _mm_res(y, w_out.astype(BF16), jnp.zeros((1, d), F32), x)


def kernel(x, norm_mix_g, norm_ffn_g, final_norm_g, conf_pw1_w, conf_pw1_b, conf_dw_w, conf_dw_b, conf_ln_g, conf_ln_b, conf_pw2_w, conf_pw2_b, ssm_in_w, ssm_conv_w, ssm_conv_b, ssm_dt_bias, ssm_a_log, ssm_d, ssm_norm_g, ssm_out_w, ffn_w_gate, ffn_w_up, ffn_w_down, moe_router_w, moe_w_gate, moe_w_up, moe_w_down):
    bsz, seq, d = x.shape
    assert norm_mix_g.shape[0] == 2, "two layers: Conformer conv + SwiGLU, then Mamba-2 + MoE"
    xf = x.reshape(bsz * seq, d)
    xf = _conformer_layer(xf, bsz, seq, norm_mix_g[0], norm_ffn_g[0], conf_pw1_w[0], conf_pw1_b[0],
                          conf_dw_w[0], conf_dw_b[0], conf_ln_g[0], conf_ln_b[0], conf_pw2_w[0],
                          conf_pw2_b[0], ffn_w_gate[0], ffn_w_up[0], ffn_w_down[0])
    xf = _mamba_mixer(xf, bsz, seq, norm_mix_g[1], ssm_in_w[0], ssm_conv_w[0], ssm_conv_b[0],
                      ssm_dt_bias[0], ssm_a_log[0], ssm_d[0], ssm_norm_g[0], ssm_out_w[0])
    out = _moe_block(xf, norm_ffn_g[1].reshape(1, d), final_norm_g.reshape(1, d),
                     moe_router_w[0], moe_w_gate[0].astype(BF16), moe_w_up[0].astype(BF16),
                     moe_w_down[0].astype(BF16))
    return out.reshape(bsz, seq, d)
```

```python
import functools

import jax
import jax.numpy as jnp
from jax import lax
from jax.experimental import pallas as pl
from jax.experimental.pallas import tpu as pltpu

EPS = 1e-6
SSM_GROUPS = 8
SSM_CHUNK = 256
TOP_K = 2
LANES = 128
SUBLANES = 8
MXU_COLS = 256
LOG2_E = 1.4426950408889634
VMEM_LIMIT = 56 << 20
RESIDENT_WEIGHT_BYTES = 8 << 20

F32 = jnp.float32
BF16 = jnp.bfloat16


def _cparams(*sem):
    return pltpu.CompilerParams(dimension_semantics=sem, vmem_limit_bytes=VMEM_LIMIT)


def _sigmoid(v):
    return 1.0 / (1.0 + jnp.exp(-v))


def _silu(v):
    return v * _sigmoid(v)


def _rms(x, g):
    return x * lax.rsqrt(jnp.mean(x * x, axis=-1, keepdims=True) + EPS) * g


def _dot(a, b):
    return jnp.dot(a, b, preferred_element_type=F32)


def _dot_nt(a, b):
    return lax.dot_general(a, b, (((1,), (1,)), ((), ())), preferred_element_type=F32)


def _blk(n, want):
    b = min(n, want)
    assert n % b == 0, (n, want)
    return b


def _norm_glu_kernel(x_ref, g_ref, wa_ref, wb_ref, ba_ref, bb_ref, o_ref, h_ref):
    @pl.when(pl.program_id(1) == 0)
    def _():
        h_ref[...] = _rms(x_ref[...], g_ref[...]).astype(BF16)

    h = h_ref[...]
    a = _dot(h, wa_ref[...]) + ba_ref[...]
    b = _dot(h, wb_ref[...]) + bb_ref[...]
    o_ref[...] = (a * _sigmoid(b)).astype(o_ref.dtype)


def _norm_glu(x, g, w, b, out_dtype):
    m, k = x.shape
    n = w.shape[1] // 2
    bm, bn = _blk(m, 1024), _blk(n, 1024)
    nb = n // bn
    return pl.pallas_call(
        _norm_glu_kernel,
        out_shape=jax.ShapeDtypeStruct((m, n), out_dtype),
        grid=(m // bm, nb),
        in_specs=[
            pl.BlockSpec((bm, k), lambda i, j: (i, 0)),
            pl.BlockSpec((1, k), lambda i, j: (0, 0)),
            pl.BlockSpec((k, bn), lambda i, j: (0, j)),
            pl.BlockSpec((k, bn), lambda i, j: (0, j + nb)),
            pl.BlockSpec((1, bn), lambda i, j: (0, j)),
            pl.BlockSpec((1, bn), lambda i, j: (0, j + nb)),
        ],
        out_specs=pl.BlockSpec((bm, bn), lambda i, j: (i, j)),
        scratch_shapes=[pltpu.VMEM((bm, k), BF16)],
        compiler_params=_cparams("parallel", "arbitrary"),
        name="norm_glu",
    )(x, g, w, w, b, b)


def _norm_swiglu_kernel(x_ref, g_ref, wg_ref, wu_ref, o_ref, h_ref):
    @pl.when(pl.program_id(1) == 0)
    def _():
        h_ref[...] = _rms(x_ref[...], g_ref[...]).astype(BF16)

    h = h_ref[...]
    a = _dot(h, wg_ref[...])
    b = _dot(h, wu_ref[...])
    o_ref[...] = (_silu(a) * b).astype(o_ref.dtype)


def _norm_swiglu(x, g, wg, wu):
    m, k = x.shape
    n = wg.shape[1]
    bm, bn = _blk(m, 1024), _blk(n, 1024)
    return pl.pallas_call(
        _norm_swiglu_kernel,
        out_shape=jax.ShapeDtypeStruct((m, n), BF16),
        grid=(m // bm, n // bn),
        in_specs=[
            pl.BlockSpec((bm, k), lambda i, j: (i, 0)),
            pl.BlockSpec((1, k), lambda i, j: (0, 0)),
            pl.BlockSpec((k, bn), lambda i, j: (0, j)),
            pl.BlockSpec((k, bn), lambda i, j: (0, j)),
        ],
        out_specs=pl.BlockSpec((bm, bn), lambda i, j: (i, j)),
        scratch_shapes=[pltpu.VMEM((bm, k), BF16)],
        compiler_params=_cparams("parallel", "arbitrary"),
        name="norm_swiglu",
    )(x, g, wg, wu)


def _norm_inproj_kernel(x_ref, g_ref, w_ref, wdt_ref, o_ref, dt_ref, h_ref):
    @pl.when(pl.program_id(1) == 0)
    def _():
        h = _rms(x_ref[...], g_ref[...]).astype(BF16)
        h_ref[...] = h
        dt_ref[...] = _dot(h, wdt_ref[...])

    o_ref[...] = _dot(h_ref[...], w_ref[...])


def _norm_inproj(x, g, w, wdt):
    m, k = x.shape
    n = w.shape[1]
    bm, bn = _blk(m, 1024), _blk(n, 1024)
    return pl.pallas_call(
        _norm_inproj_kernel,
        out_shape=(jax.ShapeDtypeStruct((m, n), F32),
                   jax.ShapeDtypeStruct((m, wdt.shape[1]), F32)),
        grid=(m // bm, n // bn),
        in_specs=[
            pl.BlockSpec((bm, k), lambda i, j: (i, 0)),
            pl.BlockSpec((1, k), lambda i, j: (0, 0)),
            pl.BlockSpec((k, bn), lambda i, j: (0, j)),
            pl.BlockSpec((k, wdt.shape[1]), lambda i, j: (0, 0)),
        ],
        out_specs=(pl.BlockSpec((bm, bn), lambda i, j: (i, j)),
                   pl.BlockSpec((bm, wdt.shape[1]), lambda i, j: (i, 0))),
        scratch_shapes=[pltpu.VMEM((bm, k), BF16)],
        compiler_params=_cparams("parallel", "arbitrary"),
        name="norm_inproj",
    )(x, g, w, wdt)


def _mm_res_kernel(a_ref, w_ref, b_ref, r_ref, o_ref):
    o_ref[...] = r_ref[...] + _dot(a_ref[...], w_ref[...]) + b_ref[...]


def _mm_res(a, w, bias, res):
    m, k = a.shape
    n = w.shape[1]
    if k * n * 2 <= RESIDENT_WEIGHT_BYTES:
        bm, bn = _blk(m, 512), n
    else:
        bm, bn = _blk(m, 1024), _blk(n, 256 if k > 4096 else 512)
    return pl.pallas_call(
        _mm_res_kernel,
        out_shape=jax.ShapeDtypeStruct((m, n), F32),
        grid=(m // bm, n // bn),
        in_specs=[
            pl.BlockSpec((bm, k), lambda i, j: (i, 0)),
            pl.BlockSpec((k, bn), lambda i, j: (0, j)),
            pl.BlockSpec((1, bn), lambda i, j: (0, j)),
            pl.BlockSpec((bm, bn), lambda i, j: (i, j)),
        ],
        out_specs=pl.BlockSpec((bm, bn), lambda i, j: (i, j)),
        compiler_params=_cparams("parallel", "arbitrary"),
        name="mm_res",
    )(a, w, bias, res)


CONV_ROWS = 16


def _dwconv_kernel(*refs, taps, halo, bt, cb, layer_norm):
    if layer_norm:
        xc_ref, xh_ref, w_ref, b_ref, lg_ref, lb_ref, o_ref, win_ref, y_ref = refs
    else:
        xc_ref, xh_ref, w_ref, b_ref, o_ref, win_ref = refs
        y_ref = None
    first = pl.program_id(1) == 0
    win_ref[0, 0:halo, :] = jnp.where(first, 0.0, xh_ref[...])
    win_ref[0, halo:halo + bt, :] = xc_ref[...]
    off = halo - (taps - 1)
    n_sh = halo + bt - SUBLANES
    for s in sorted({(off + j) % SUBLANES for j in range(taps)} - {0}):
        win_ref[s, 0:n_sh, :] = win_ref[0, s:s + n_sh, :]

    def body(r, carry):
        r0 = pl.multiple_of(r * CONV_ROWS, CONV_ROWS)
        for c in range(cb // LANES):
            cs = slice(c * LANES, (c + 1) * LANES)
            acc = jnp.broadcast_to(b_ref[:, cs], (CONV_ROWS, LANES))
            for j in range(taps):
                al, s = divmod(off + j, SUBLANES)
                xw = win_ref[s, pl.ds(r0 + al * SUBLANES, CONV_ROWS), cs]
                acc = acc + xw * w_ref[j:j + 1, cs]
            if layer_norm:
                y_ref[pl.ds(r0, CONV_ROWS), cs] = acc
            else:
                o_ref[pl.ds(r0, CONV_ROWS), cs] = _silu(acc).astype(o_ref.dtype)
        return carry

    lax.fori_loop(0, bt // CONV_ROWS, body, 0)

    if layer_norm:
        y = y_ref[...]
        mu = jnp.mean(y, axis=-1, keepdims=True)
        d = y - mu
        var = jnp.mean(d * d, axis=-1, keepdims=True)
        hn = d * lax.rsqrt(var + EPS) * lg_ref[...] + lb_ref[...]
        o_ref[...] = _silu(hn).astype(o_ref.dtype)


def _dwconv(x3, w, b, ln, out_dtype, c_start=0):
    bsz, seq, _ = x3.shape
    taps, ch = w.shape
    halo = SUBLANES * (-(-(taps - 1) // SUBLANES))
    bt = _blk(seq, 256)
    assert bt % halo == 0 and bt % CONV_ROWS == 0
    layer_norm = ln is not None
    cb = ch if layer_norm else _blk(ch, 1024)
    assert c_start % cb == 0
    co = c_start // cb
    hb = bt // halo
    in_specs = [
        pl.BlockSpec((None, bt, cb), lambda bi, i, c: (bi, i, c + co)),
        pl.BlockSpec((None, halo, cb),
                     lambda bi, i, c: (bi, jnp.maximum(i * hb - 1, 0), c + co)),
        pl.BlockSpec((taps, cb), lambda bi, i, c: (0, c)),
        pl.BlockSpec((1, cb), lambda bi, i, c: (0, c)),
    ]
    args = [x3, x3, w, b]
    scratch = [pltpu.VMEM((SUBLANES, halo + bt, cb), F32)]
    if layer_norm:
        in_specs += [pl.BlockSpec((1, cb), lambda bi, i, c: (0, c))] * 2
        args += [ln[0], ln[1]]
        scratch.append(pltpu.VMEM((bt, cb), F32))
    return pl.pallas_call(
        functools.partial(_dwconv_kernel, taps=taps, halo=halo, bt=bt, cb=cb,
                          layer_norm=layer_norm),
        out_shape=jax.ShapeDtypeStruct((bsz, seq, ch), out_dtype),
        grid=(bsz, seq // bt, ch // cb),
        in_specs=in_specs,
        out_specs=pl.BlockSpec((None, bt, cb), lambda bi, i, c: (bi, i, c)),
        scratch_shapes=scratch,
        compiler_params=_cparams("parallel", "parallel", "parallel"),
        name="dwconv_ln" if layer_norm else "dwconv",
    )(*args)


SSD_ROWS = 64


def _split3(v):
    hi = v.astype(BF16)
    r = v - hi.astype(F32)
    mid = r.astype(BF16)
    lo = (r - mid.astype(F32)).astype(BF16)
    return hi, mid, lo


def _expand_heads(v, hpg, hd):
    rows = v.shape[0]
    lane_head = lax.broadcasted_iota(jnp.int32, (rows, hpg * hd), 1) // hd
    out = jnp.zeros((rows, hpg * hd), F32)
    for h in range(hpg):
        out = jnp.where(lane_head == h, v[:, h:h + 1], out)
    return out


def _ssd_kernel(x_ref, b_ref, c_ref, z_ref, dtt_ref, dtb_ref, alog_ref, dexp_ref, ng_ref,
                o_ref, s_ref, y_ref, *, hpg, hd, q):
    @pl.when(pl.program_id(2) == 0)
    def _():
        s_ref[...] = jnp.zeros_like(s_ref)

    xg = x_ref[...]
    bg = b_ref[...]
    cg = c_ref[...]
    cgb = cg.astype(BF16)
    v = dtt_ref[...] + dtb_ref[...]
    dt = jnp.maximum(v, 0.0) + jnp.log1p(jnp.exp(-jnp.abs(v)))
    a = dt * (-LOG2_E * jnp.exp(alog_ref[...]))

    a_pad = jnp.concatenate([a, jnp.zeros((LANES - hpg, q), F32)], axis=0)
    rr = lax.broadcasted_iota(jnp.int32, (q, q), 0)
    cc = lax.broadcasted_iota(jnp.int32, (q, q), 1)
    causal = rr >= cc
    tri_ls = jnp.where(causal, 1.0, 0.0).astype(BF16)
    tri_sl = jnp.where(rr <= cc, 1.0, 0.0).astype(BF16)
    acs_t = jnp.zeros((LANES, q), F32)
    acs = jnp.zeros((q, LANES), F32)
    for piece in _split3(a_pad):
        acs_t = acs_t + _dot(piece, tri_sl)
        acs = acs + _dot_nt(tri_ls, piece)
    acs_t = acs_t[0:hpg, :]
    wst_t = jnp.exp2(acs_t[:, q - 1:q] - acs_t) * dt

    cb = _dot_nt(cgb, bg.astype(BF16))
    bt = bg.T
    pair = LANES // hd
    lane_y = lax.broadcasted_iota(jnp.int32, (SSD_ROWS, LANES), 1)
    lane_s = lax.broadcasted_iota(jnp.int32, (bt.shape[0], LANES), 1)
    s_old = s_ref[...]
    for t in range(hpg // pair):
        ts = slice(t * LANES, (t + 1) * LANES)
        x_t = xg[:, ts].astype(BF16)
        s_t = None
        for p in range(pair):
            h = t * pair + p
            s_h = _dot((bt * wst_t[h:h + 1, :]).astype(BF16), x_t)
            s_t = s_h if p == 0 else jnp.where(lane_s >= p * hd, s_h, s_t)
        for rc in range(q // SSD_ROWS):
            rs = slice(rc * SSD_ROWS, (rc + 1) * SSD_ROWS)
            kc = min(q, LANES * (-(-(rc + 1) * SSD_ROWS // LANES)))
            causal_c = (lax.broadcasted_iota(jnp.int32, (SSD_ROWS, kc), 0) + rc * SSD_ROWS
                        >= lax.broadcasted_iota(jnp.int32, (SSD_ROWS, kc), 1))
            y_c = None
            for p in range(pair):
                h = t * pair + p
                seg = acs[rs, h:h + 1] - acs_t[h:h + 1, :kc]
                decay = jnp.exp2(jnp.where(causal_c, seg, -jnp.inf))
                m = (cb[rs, :kc] * decay * dt[h:h + 1, :kc]).astype(BF16)
                y_h = _dot(m, x_t[:kc])
                y_c = y_h if p == 0 else jnp.where(lane_y >= p * hd, y_h, y_c)
            y_ref[rs, ts] = y_c
        dec_t = _expand_heads(jnp.exp2(acs[q - 1:q, t * pair:(t + 1) * pair]), pair, hd)
        s_ref[:, ts] = s_old[:, ts] * dec_t + s_t

    y = y_ref[...]
    y = y + _dot(cgb, s_old.astype(BF16)) * _expand_heads(jnp.exp2(acs[:, 0:hpg]), hpg, hd)
    y = y + xg * dexp_ref[...]
    y = y * _silu(z_ref[...])
    y = y * lax.rsqrt(jnp.mean(y * y, axis=-1, keepdims=True) + EPS) * ng_ref[...]
    o_ref[...] = y.astype(o_ref.dtype)


def _ssd(zx, xbc, dtt, dt_bias, a_log, d_exp, norm_g, *, bsz, seq, d_inner, n_state):
    groups = SSM_GROUPS
    heads = dtt.shape[0]
    hpg = heads // groups
    hd = d_inner // heads
    gw = hpg * hd
    q = _blk(seq, SSM_CHUNK)
    nc = seq // q
    assert gw % LANES == 0 and LANES % hd == 0 and n_state % LANES == 0
    assert d_inner // groups == gw
    b_off = d_inner // n_state
    c_off = b_off + groups
    row = lambda b, g, c: b * nc + c
    return pl.pallas_call(
        functools.partial(_ssd_kernel, hpg=hpg, hd=hd, q=q),
        out_shape=jax.ShapeDtypeStruct((bsz * seq, d_inner), BF16),
        grid=(bsz, groups, nc),
        in_specs=[
            pl.BlockSpec((q, gw), lambda b, g, c: (row(b, g, c), g)),
            pl.BlockSpec((q, n_state), lambda b, g, c: (row(b, g, c), b_off + g)),
            pl.BlockSpec((q, n_state), lambda b, g, c: (row(b, g, c), c_off + g)),
            pl.BlockSpec((q, gw), lambda b, g, c: (row(b, g, c), g)),
            pl.BlockSpec((hpg, q), lambda b, g, c: (g, row(b, g, c))),
            pl.BlockSpec((hpg, 1), lambda b, g, c: (g, 0)),
            pl.BlockSpec((hpg, 1), lambda b, g, c: (g, 0)),
            pl.BlockSpec((1, gw), lambda b, g, c: (0, g)),
            pl.BlockSpec((1, gw), lambda b, g, c: (0, g)),
        ],
        out_specs=pl.BlockSpec((q, gw), lambda b, g, c: (row(b, g, c), g)),
        scratch_shapes=[pltpu.VMEM((n_state, gw), F32), pltpu.VMEM((q, gw), F32)],
        compiler_params=_cparams("parallel", "parallel", "arbitrary"),
        name="ssd",
    )(xbc, xbc, xbc, zx, dtt, dt_bias, a_log, d_exp, norm_g)


def _router_kernel(x_ref, g_ref, wr_ref, idx_ref, gw_ref, *, n_experts):
    h = _rms(x_ref[...], g_ref[...])
    w = wr_ref[...]
    hh = h.astype(BF16)
    hl = (h - hh.astype(F32)).astype(BF16)
    wh = w.astype(BF16)
    wl = (w - wh.astype(F32)).astype(BF16)
    logits = _dot(hh, wh) + _dot(hl, wh) + _dot(hh, wl)
    lane = lax.broadcasted_iota(jnp.int32, logits.shape, 1).astype(F32)
    neg = -jnp.inf
    lg = jnp.where(lane < n_experts, logits, neg)
    m1 = jnp.max(lg, axis=-1, keepdims=True)
    i1 = jnp.min(jnp.where(lg == m1, lane, float(LANES)), axis=-1, keepdims=True)
    lg2 = jnp.where(lane == i1, neg, lg)
    m2 = jnp.max(lg2, axis=-1, keepdims=True)
    i2 = jnp.min(jnp.where(lg2 == m2, lane, float(LANES)), axis=-1, keepdims=True)
    e = jnp.exp(m2 - m1)
    w1 = 1.0 / (1.0 + e)
    w2 = e / (1.0 + e)
    idx_ref[...] = jnp.where(lane == 0.0, i1, i2).astype(jnp.int32)
    gw_ref[...] = jnp.where(lane == 0.0, w1, w2)


def _router(x, g, wr, n_experts):
    m, k = x.shape
    bm = _blk(m, 512)
    return pl.pallas_call(
        functools.partial(_router_kernel, n_experts=n_experts),
        out_shape=(jax.ShapeDtypeStruct((m, LANES), jnp.int32),
                   jax.ShapeDtypeStruct((m, LANES), F32)),
        grid=(m // bm,),
        in_specs=[
            pl.BlockSpec((bm, k), lambda i: (i, 0)),
            pl.BlockSpec((1, k), lambda i: (0, 0)),
            pl.BlockSpec((k, LANES), lambda i: (0, 0)),
        ],
        out_specs=(pl.BlockSpec((bm, LANES), lambda i: (i, 0)),
                   pl.BlockSpec((bm, LANES), lambda i: (i, 0))),
        compiler_params=_cparams("parallel"),
        name="router",
    )(x, g, wr)


def _rank_kernel(e_ref, rank_ref, cnt_ref, carry_ref, *, ep, bl):
    @pl.when(pl.program_id(0) == 0)
    def _():
        carry_ref[...] = jnp.zeros_like(carry_ref)

    e_row = e_ref[0]
    sub = lax.broadcasted_iota(jnp.int32, (ep, bl), 0)
    hit = sub == e_row
    oh = jnp.where(hit, 1.0, 0.0).astype(BF16)
    rr = lax.broadcasted_iota(jnp.int32, (bl, bl), 0)
    cc = lax.broadcasted_iota(jnp.int32, (bl, bl), 1)
    tri = jnp.where(rr <= cc, 1.0, 0.0).astype(BF16)
    pre = _dot(oh, tri)
    carry = carry_ref[...]
    rank = jnp.sum(jnp.where(hit, pre - 1.0 + carry[:, 0:1], 0.0), axis=0, keepdims=True)
    rank_ref[0] = rank.astype(jnp.int32)
    carry = carry + pre[:, bl - 1:bl]
    carry_ref[...] = carry
    cnt_ref[...] = carry.astype(jnp.int32)


def _ranks(e_flat, n_experts):
    n = e_flat.shape[0]
    bl = _blk(n, 512)
    ep = 2 * SUBLANES * (-(-n_experts // (2 * SUBLANES)))
    rank, cnt = pl.pallas_call(
        functools.partial(_rank_kernel, ep=ep, bl=bl),
        out_shape=(jax.ShapeDtypeStruct((n // bl, 1, bl), jnp.int32),
                   jax.ShapeDtypeStruct((ep, LANES), jnp.int32)),
        grid=(n // bl,),
        in_specs=[pl.BlockSpec((1, 1, bl), lambda i: (i, 0, 0))],
        out_specs=(pl.BlockSpec((1, 1, bl), lambda i: (i, 0, 0)),
                   pl.BlockSpec((ep, LANES), lambda i: (0, 0))),
        scratch_shapes=[pltpu.VMEM((ep, LANES), F32)],
        compiler_params=_cparams("arbitrary"),
        name="expert_ranks",
    )(e_flat.reshape(n // bl, 1, bl))
    return rank.reshape(n), cnt[:n_experts, 0]


def _row_copy(src, dst, s_row, d_row, sem):
    return pltpu.make_async_copy(src.at[pl.ds(s_row, 1)], dst.at[pl.ds(d_row, 1)], sem)


def _scatter_kernel(pos_ref, x_ref, g_ref, xs_in_ref, xs_ref, h_ref, sem, *, rb, n_tok):
    del xs_in_ref
    base = pl.program_id(0) * rb
    h_ref[...] = _rms(x_ref[...], g_ref[...])

    def issue(r, carry):
        for k in range(TOP_K):
            _row_copy(h_ref, xs_ref, r, pos_ref[k * n_tok + base + r], sem).start()
        return carry

    lax.fori_loop(0, rb, issue, 0, unroll=8)

    def drain(r, carry):
        for k in range(TOP_K):
            _row_copy(h_ref, xs_ref, 0, 0, sem).wait()
        return carry

    lax.fori_loop(0, rb, drain, 0, unroll=8)


def _scatter_rows(pos, x, g, n_rows):
    n_tok, k = x.shape
    rb = _blk(n_tok, 256)
    xs0 = jnp.zeros((n_rows, k), F32)
    return pl.pallas_call(
        functools.partial(_scatter_kernel, rb=rb, n_tok=n_tok),
        out_shape=jax.ShapeDtypeStruct((n_rows, k), F32),
        grid_spec=pltpu.PrefetchScalarGridSpec(
            num_scalar_prefetch=1,
            grid=(n_tok // rb,),
            in_specs=[pl.BlockSpec((rb, k), lambda i, pos: (i, 0)),
                      pl.BlockSpec((1, k), lambda i, pos: (0, 0)),
                      pl.BlockSpec(memory_space=pl.ANY)],
            out_specs=pl.BlockSpec(memory_space=pl.ANY),
            scratch_shapes=[pltpu.VMEM((rb, k), F32), pltpu.SemaphoreType.DMA],
        ),
        input_output_aliases={3: 0},
        compiler_params=_cparams("arbitrary"),
        name="moe_scatter",
    )(pos, x, g, xs0)


def _moe_up_kernel(te_ref, nu_ref, xs_ref, wg_ref, wu_ref, o_ref):
    del te_ref

    @pl.when(pl.program_id(1) < nu_ref[0])
    def _():
        h = xs_ref[...].astype(BF16)
        for c in range(o_ref.shape[1] // MXU_COLS):
            cs = slice(c * MXU_COLS, (c + 1) * MXU_COLS)
            a = _dot(h, wg_ref[:, cs].astype(BF16))
            b = _dot(h, wu_ref[:, cs].astype(BF16))
            o_ref[:, cs] = (_silu(a) * b).astype(o_ref.dtype)

    @pl.when(pl.program_id(1) >= nu_ref[0])
    def _():
        o_ref[...] = jnp.zeros_like(o_ref)


def _moe_up(te, nu, xs, wg, wu, bm):
    rows, k = xs.shape
    f = wg.shape[2]
    bn = _blk(f, 1024)
    return pl.pallas_call(
        _moe_up_kernel,
        out_shape=jax.ShapeDtypeStruct((rows, f), BF16),
        grid_spec=pltpu.PrefetchScalarGridSpec(
            num_scalar_prefetch=2,
            grid=(f // bn, rows // bm),
            in_specs=[
                pl.BlockSpec((bm, k), lambda j, i, te, nu: (i, 0)),
                pl.BlockSpec((None, k, bn), lambda j, i, te, nu: (te[i], 0, j)),
                pl.BlockSpec((None, k, bn), lambda j, i, te, nu: (te[i], 0, j)),
            ],
            out_specs=pl.BlockSpec((bm, bn), lambda j, i, te, nu: (i, j)),
        ),
        compiler_params=_cparams("parallel", "arbitrary"),
        name="moe_up",
    )(te, nu, xs, wg, wu)


def _moe_down_kernel(te_ref, nu_ref, a_ref, wd_ref, o_ref):
    del te_ref

    @pl.when(pl.program_id(1) < nu_ref[0])
    def _():
        a = a_ref[...]
        for c in range(o_ref.shape[1] // MXU_COLS):
            cs = slice(c * MXU_COLS, (c + 1) * MXU_COLS)
            o_ref[:, cs] = _dot(a, wd_ref[:, cs].astype(BF16))

    @pl.when(pl.program_id(1) >= nu_ref[0])
    def _():
        o_ref[...] = jnp.zeros_like(o_ref)


def _moe_down(te, nu, act, wd, bm):
    rows, f = act.shape
    n = wd.shape[2]
    bn = _blk(n, 512)
    return pl.pallas_call(
        _moe_down_kernel,
        out_shape=jax.ShapeDtypeStruct((rows, n), F32),
        grid_spec=pltpu.PrefetchScalarGridSpec(
            num_scalar_prefetch=2,
            grid=(n // bn, rows // bm),
            in_specs=[
                pl.BlockSpec((bm, f), lambda j, i, te, nu: (i, 0)),
                pl.BlockSpec((None, f, bn), lambda j, i, te, nu: (te[i], 0, j)),
            ],
            out_specs=pl.BlockSpec((bm, bn), lambda j, i, te, nu: (i, j)),
        ),
        compiler_params=_cparams("parallel", "arbitrary"),
        name="moe_down",
    )(te, nu, act, wd)


def _combine_kernel(pos_ref, x_ref, gw_ref, fg_ref, ys_ref, o_ref, buf_ref, sem, *, rb, n_tok):
    base = pl.program_id(0) * rb

    def issue(r, carry):
        t = base + r
        for k in range(TOP_K):
            _row_copy(ys_ref, buf_ref.at[k], pos_ref[k * n_tok + t], r, sem).start()
        return carry

    lax.fori_loop(0, rb, issue, 0, unroll=8)

    def drain(r, carry):
        for k in range(TOP_K):
            _row_copy(ys_ref, buf_ref.at[k], 0, 0, sem).wait()
        return carry

    lax.fori_loop(0, rb, drain, 0, unroll=8)

    gw = gw_ref[...]
    y = x_ref[...] + gw[:, 0:1] * buf_ref[0] + gw[:, 1:2] * buf_ref[1]
    o_ref[...] = _rms(y, fg_ref[...])


def _combine_norm(pos, x, gw, fg, ys):
    n_tok, d = x.shape
    rb = _blk(n_tok, 256)
    return pl.pallas_call(
        functools.partial(_combine_kernel, rb=rb, n_tok=n_tok),
        out_shape=jax.ShapeDtypeStruct((n_tok, d), F32),
        grid_spec=pltpu.PrefetchScalarGridSpec(
            num_scalar_prefetch=1,
            grid=(n_tok // rb,),
            in_specs=[
                pl.BlockSpec((rb, d), lambda i, pos: (i, 0)),
                pl.BlockSpec((rb, LANES), lambda i, pos: (i, 0)),
                pl.BlockSpec((1, d), lambda i, pos: (0, 0)),
                pl.BlockSpec(memory_space=pl.ANY),
            ],
            out_specs=pl.BlockSpec((rb, d), lambda i, pos: (i, 0)),
            scratch_shapes=[pltpu.VMEM((TOP_K, rb, d), F32), pltpu.SemaphoreType.DMA],
        ),
        compiler_params=_cparams("arbitrary"),
        name="moe_combine_norm",
    )(pos, x, gw, fg, ys)


def _moe_block(x, norm_g, final_g, w_router, w_gate, w_up, w_down):
    n_tok, d = x.shape
    n_experts = w_router.shape[1]
    bm = _blk(n_tok, 512)
    wr = jnp.zeros((d, LANES), F32).at[:, :n_experts].set(w_router)
    idx, gw = _router(x, norm_g, wr, n_experts)

    e_flat = idx[:, :TOP_K].T.reshape(TOP_K * n_tok)
    rank, counts = _ranks(e_flat, n_experts)

    tiles = (counts + bm - 1) // bm
    tile_end = jnp.cumsum(tiles)
    row_off = (tile_end - tiles) * bm
    pos = row_off[e_flat] + rank
    n_tiles = TOP_K * n_tok // bm + n_experts
    n_used = tile_end[-1]
    tile_ids = jnp.minimum(jnp.arange(n_tiles, dtype=jnp.int32), n_used - 1)
    te = jnp.sum(tile_ids[:, None] >= tile_end[None, :], axis=1).astype(jnp.int32)
    nu = n_used.reshape(1).astype(jnp.int32)

    xs = _scatter_rows(pos, x, norm_g, n_tiles * bm)
    act = _moe_up(te, nu, xs, w_gate, w_up, bm)
    ys = _moe_down(te, nu, act, w_down, bm)
    return _combine_norm(pos, x, gw, final_g, ys)


def _conformer_layer(x, bsz, seq, mix_g, ffn_g, pw1_w, pw1_b, dw_w, dw_b, ln_g, ln_b,
                     pw2_w, pw2_b, w_gate, w_up, w_down):
    d = x.shape[1]
    row = lambda v: v.reshape(1, -1)
    glu = _norm_glu(x, row(mix_g), pw1_w.astype(BF16), row(pw1_b), F32)
    hc = _dwconv(glu.reshape(bsz, seq, d), dw_w, row(dw_b), (row(ln_g), row(ln_b)), BF16)
    x = _mm_res(hc.reshape(bsz * seq, d), pw2_w.astype(BF16), row(pw2_b), x)
    act = _norm_swiglu(x, row(ffn_g), w_gate.astype(BF16), w_up.astype(BF16))
    return _mm_res(act, w_down.astype(BF16), jnp.zeros((1, d), F32), x)


def _mamba_mixer(x, bsz, seq, mix_g, w_in, conv_w, conv_b, dt_bias, a_log, d_skip, norm_g, w_out):
    d = x.shape[1]
    heads = dt_bias.shape[0]
    d_inner = norm_g.shape[0]
    conv_dim = conv_w.shape[1]
    n_state = (conv_dim - d_inner) // (2 * SSM_GROUPS)
    hd = d_inner // heads
    row = lambda v: v.reshape(1, -1)
    n_main = d_inner + conv_dim
    w_main = w_in[:, :n_main].astype(BF16)
    w_dt = jnp.zeros((d, LANES), BF16).at[:, :heads].set(w_in[:, n_main:].astype(BF16))
    zx, dt_raw = _norm_inproj(x, row(mix_g), w_main, w_dt)
    xbc = _dwconv(zx.reshape(bsz, seq, n_main), conv_w, row(conv_b), None, F32, c_start=d_inner)
    y = _ssd(zx, xbc.reshape(bsz * seq, conv_dim), dt_raw[:, :heads].T,
             dt_bias.reshape(heads, 1), a_log.reshape(heads, 1),
             row(jnp.repeat(d_skip, hd)), row(norm_g),
             bsz=bsz, seq=seq, d_inner=d_inner, n_state=n_state)
    return _mm_res(y, w_out.astype(BF16), jnp.zeros((1, d), F32), x)


def kernel(x, norm_mix_g, norm_ffn_g, final_norm_g, conf_pw1_w, conf_pw1_b, conf_dw_w, conf_dw_b, conf_ln_g, conf_ln_b, conf_pw2_w, conf_pw2_b, ssm_in_w, ssm_conv_w, ssm_conv_b, ssm_dt_bias, ssm_a_log, ssm_d, ssm_norm_g, ssm_out_w, ffn_w_gate, ffn_w_up, ffn_w_down, moe_router_w, moe_w_gate, moe_w_up, moe_w_down):
    bsz, seq, d = x.shape
    assert norm_mix_g.shape[0] == 2, "two layers: Conformer conv + SwiGLU, then Mamba-2 + MoE"
    xf = x.reshape(bsz * seq, d)
    xf = _conformer_layer(xf, bsz, seq, norm_mix_g[0], norm_ffn_g[0], conf_pw1_w[0], conf_pw1_b[0],
                          conf_dw_w[0], conf_dw_b[0], conf_ln_g[0], conf_ln_b[0], conf_pw2_w[0],
                          conf_pw2_b[0], ffn_w_gate[0], ffn_w_up[0], ffn_w_down[0])
    xf = _mamba_mixer(xf, bsz, seq, norm_mix_g[1], ssm_in_w[0], ssm_conv_w[0], ssm_conv_b[0],
                      ssm_dt_bias[0], ssm_a_log[0], ssm_d[0], ssm_norm_g[0], ssm_out_w[0])
    out = _moe_block(xf, norm_ffn_g[1].reshape(1, d), final_norm_g.reshape(1, d),
                     moe_router_w[0], moe_w_gate.reshape(moe_w_gate.shape[1:]),
                     moe_w_up.reshape(moe_w_up.shape[1:]), moe_w_down.reshape(moe_w_down.shape[1:]))
    return out.reshape(bsz, seq, d)
```

```python
import functools

import jax
import jax.numpy as jnp
from jax import lax
from jax.experimental import pallas as pl
from jax.experimental.pallas import tpu as pltpu

EPS = 1e-6
SSM_GROUPS = 8
SSM_CHUNK = 256
TOP_K = 2
LANES = 128
SUBLANES = 8
MXU_COLS = 256
LOG2_E = 1.4426950408889634
VMEM_LIMIT = 56 << 20
RESIDENT_WEIGHT_BYTES = 8 << 20

F32 = jnp.float32
BF16 = jnp.bfloat16


def _cparams(*sem):
    return pltpu.CompilerParams(dimension_semantics=sem, vmem_limit_bytes=VMEM_LIMIT)


def _sigmoid(v):
    return 1.0 / (1.0 + jnp.exp(-v))


def _silu(v):
    return v * _sigmoid(v)


def _rms(x, g):
    return x * lax.rsqrt(jnp.mean(x * x, axis=-1, keepdims=True) + EPS) * g


def _dot(a, b):
    return jnp.dot(a, b, preferred_element_type=F32)


def _dot_nt(a, b):
    return lax.dot_general(a, b, (((1,), (1,)), ((), ())), preferred_element_type=F32)


def _blk(n, want):
    b = min(n, want)
    assert n % b == 0, (n, want)
    return b


def _norm_glu_kernel(x_ref, g_ref, wa_ref, wb_ref, ba_ref, bb_ref, o_ref, h_ref):
    @pl.when(pl.program_id(1) == 0)
    def _():
        h_ref[...] = _rms(x_ref[...], g_ref[...]).astype(BF16)

    h = h_ref[...]
    a = _dot(h, wa_ref[...]) + ba_ref[...]
    b = _dot(h, wb_ref[...]) + bb_ref[...]
    o_ref[...] = (a * _sigmoid(b)).astype(o_ref.dtype)


def _norm_glu(x, g, w, b, out_dtype):
    m, k = x.shape
    n = w.shape[1] // 2
    bm, bn = _blk(m, 1024), _blk(n, 1024)
    nb = n // bn
    return pl.pallas_call(
        _norm_glu_kernel,
        out_shape=jax.ShapeDtypeStruct((m, n), out_dtype),
        grid=(m // bm, nb),
        in_specs=[
            pl.BlockSpec((bm, k), lambda i, j: (i, 0)),
            pl.BlockSpec((1, k), lambda i, j: (0, 0)),
            pl.BlockSpec((k, bn), lambda i, j: (0, j)),
            pl.BlockSpec((k, bn), lambda i, j: (0, j + nb)),
            pl.BlockSpec((1, bn), lambda i, j: (0, j)),
            pl.BlockSpec((1, bn), lambda i, j: (0, j + nb)),
        ],
        out_specs=pl.BlockSpec((bm, bn), lambda i, j: (i, j)),
        scratch_shapes=[pltpu.VMEM((bm, k), BF16)],
        compiler_params=_cparams("parallel", "arbitrary"),
        name="norm_glu",
    )(x, g, w, w, b, b)


def _norm_swiglu_kernel(x_ref, g_ref, wg_ref, wu_ref, o_ref, h_ref):
    @pl.when(pl.program_id(1) == 0)
    def _():
        h_ref[...] = _rms(x_ref[...], g_ref[...]).astype(BF16)

    h = h_ref[...]
    a = _dot(h, wg_ref[...])
    b = _dot(h, wu_ref[...])
    o_ref[...] = (_silu(a) * b).astype(o_ref.dtype)


def _norm_swiglu(x, g, wg, wu):
    m, k = x.shape
    n = wg.shape[1]
    bm, bn = _blk(m, 1024), _blk(n, 1024)
    return pl.pallas_call(
        _norm_swiglu_kernel,
        out_shape=jax.ShapeDtypeStruct((m, n), BF16),
        grid=(m // bm, n // bn),
        in_specs=[
            pl.BlockSpec((bm, k), lambda i, j: (i, 0)),
            pl.BlockSpec((1, k), lambda i, j: (0, 0)),
            pl.BlockSpec((k, bn), lambda i, j: (0, j)),
            pl.BlockSpec((k, bn), lambda i, j: (0, j)),
        ],
        out_specs=pl.BlockSpec((bm, bn), lambda i, j: (i, j)),
        scratch_shapes=[pltpu.VMEM((bm, k), BF16)],
        compiler_params=_cparams("parallel", "arbitrary"),
        name="norm_swiglu",
    )(x, g, wg, wu)


def _norm_inproj_kernel(x_ref, g_ref, w_ref, wdt_ref, o_ref, dt_ref, h_ref):
    @pl.when(pl.program_id(1) == 0)
    def _():
        h = _rms(x_ref[...], g_ref[...]).astype(BF16)
        h_ref[...] = h
        dt_ref[...] = _dot(h, wdt_ref[...])

    o_ref[...] = _dot(h_ref[...], w_ref[...])


def _norm_inproj(x, g, w, wdt, n):
    m, k = x.shape
    bm, bn = _blk(m, 1024), _blk(n, 1024)
    return pl.pallas_call(
        _norm_inproj_kernel,
        out_shape=(jax.ShapeDtypeStruct((m, n), F32),
                   jax.ShapeDtypeStruct((m, wdt.shape[1]), F32)),
        grid=(m // bm, n // bn),
        in_specs=[
            pl.BlockSpec((bm, k), lambda i, j: (i, 0)),
            pl.BlockSpec((1, k), lambda i, j: (0, 0)),
            pl.BlockSpec((k, bn), lambda i, j: (0, j)),
            pl.BlockSpec((k, wdt.shape[1]), lambda i, j: (0, 0)),
        ],
        out_specs=(pl.BlockSpec((bm, bn), lambda i, j: (i, j)),
                   pl.BlockSpec((bm, wdt.shape[1]), lambda i, j: (i, 0))),
        scratch_shapes=[pltpu.VMEM((bm, k), BF16)],
        compiler_params=_cparams("parallel", "arbitrary"),
        name="norm_inproj",
    )(x, g, w, wdt)


def _mm_res_kernel(a_ref, w_ref, b_ref, r_ref, o_ref):
    o_ref[...] = r_ref[...] + _dot(a_ref[...], w_ref[...]) + b_ref[...]


def _mm_res(a, w, bias, res):
    m, k = a.shape
    n = w.shape[1]
    if k * n * 2 <= RESIDENT_WEIGHT_BYTES:
        bm, bn = _blk(m, 512), n
    else:
        bm, bn = _blk(m, 1024), _blk(n, 256 if k > 4096 else 512)
    return pl.pallas_call(
        _mm_res_kernel,
        out_shape=jax.ShapeDtypeStruct((m, n), F32),
        grid=(m // bm, n // bn),
        in_specs=[
            pl.BlockSpec((bm, k), lambda i, j: (i, 0)),
            pl.BlockSpec((k, bn), lambda i, j: (0, j)),
            pl.BlockSpec((1, bn), lambda i, j: (0, j)),
            pl.BlockSpec((bm, bn), lambda i, j: (i, j)),
        ],
        out_specs=pl.BlockSpec((bm, bn), lambda i, j: (i, j)),
        compiler_params=_cparams("parallel", "arbitrary"),
        name="mm_res",
    )(a, w, bias, res)


def _conv_rows(taps):
    return 16 if taps > SUBLANES else 32


def _dwconv_kernel(*refs, taps, halo, bt, cb, layer_norm):
    rows = _conv_rows(taps)
    if layer_norm:
        xc_ref, xh_ref, w_ref, b_ref, lg_ref, lb_ref, o_ref, win_ref, y_ref = refs
    else:
        xc_ref, xh_ref, w_ref, b_ref, o_ref, win_ref = refs
        y_ref = None
    first = pl.program_id(1) == 0
    win_ref[0, 0:halo, :] = jnp.where(first, 0.0, xh_ref[...])
    win_ref[0, halo:halo + bt, :] = xc_ref[...]
    off = halo - (taps - 1)
    n_sh = halo + bt - SUBLANES
    for s in sorted({(off + j) % SUBLANES for j in range(taps)} - {0}):
        win_ref[s, 0:n_sh, :] = win_ref[0, s:s + n_sh, :]

    def body(r, carry):
        r0 = pl.multiple_of(r * rows, rows)
        for c in range(cb // LANES):
            cs = slice(c * LANES, (c + 1) * LANES)
            acc = jnp.broadcast_to(b_ref[:, cs], (rows, LANES))
            for j in range(taps):
                al, s = divmod(off + j, SUBLANES)
                xw = win_ref[s, pl.ds(r0 + al * SUBLANES, rows), cs]
                acc = acc + xw * w_ref[j:j + 1, cs]
            if layer_norm:
                y_ref[pl.ds(r0, rows), cs] = acc
            else:
                o_ref[pl.ds(r0, rows), cs] = _silu(acc).astype(o_ref.dtype)
        return carry

    lax.fori_loop(0, bt // rows, body, 0)

    if layer_norm:
        y = y_ref[...]
        mu = jnp.mean(y, axis=-1, keepdims=True)
        d = y - mu
        var = jnp.mean(d * d, axis=-1, keepdims=True)
        hn = d * lax.rsqrt(var + EPS) * lg_ref[...] + lb_ref[...]
        o_ref[...] = _silu(hn).astype(o_ref.dtype)


def _dwconv(x3, w, b, ln, out_dtype, c_start=0):
    bsz, seq, _ = x3.shape
    taps, ch = w.shape
    halo = SUBLANES * (-(-(taps - 1) // SUBLANES))
    bt = _blk(seq, 256)
    assert bt % halo == 0 and bt % _conv_rows(taps) == 0
    layer_norm = ln is not None
    cb = ch if layer_norm else _blk(ch, 1024)
    assert c_start % cb == 0
    co = c_start // cb
    hb = bt // halo
    in_specs = [
        pl.BlockSpec((None, bt, cb), lambda bi, i, c: (bi, i, c + co)),
        pl.BlockSpec((None, halo, cb),
                     lambda bi, i, c: (bi, jnp.maximum(i * hb - 1, 0), c + co)),
        pl.BlockSpec((taps, cb), lambda bi, i, c: (0, c)),
        pl.BlockSpec((1, cb), lambda bi, i, c: (0, c)),
    ]
    args = [x3, x3, w, b]
    scratch = [pltpu.VMEM((SUBLANES, halo + bt, cb), F32)]
    if layer_norm:
        in_specs += [pl.BlockSpec((1, cb), lambda bi, i, c: (0, c))] * 2
        args += [ln[0], ln[1]]
        scratch.append(pltpu.VMEM((bt, cb), F32))
    return pl.pallas_call(
        functools.partial(_dwconv_kernel, taps=taps, halo=halo, bt=bt, cb=cb,
                          layer_norm=layer_norm),
        out_shape=jax.ShapeDtypeStruct((bsz, seq, ch), out_dtype),
        grid=(bsz, seq // bt, ch // cb),
        in_specs=in_specs,
        out_specs=pl.BlockSpec((None, bt, cb), lambda bi, i, c: (bi, i, c)),
        scratch_shapes=scratch,
        compiler_params=_cparams("parallel", "parallel", "parallel"),
        name="dwconv_ln" if layer_norm else "dwconv",
    )(*args)


SSD_ROWS = 64


def _split3(v):
    hi = v.astype(BF16)
    r = v - hi.astype(F32)
    mid = r.astype(BF16)
    lo = (r - mid.astype(F32)).astype(BF16)
    return hi, mid, lo


def _expand_heads(v, hpg, hd):
    rows = v.shape[0]
    lane_head = lax.broadcasted_iota(jnp.int32, (rows, hpg * hd), 1) // hd
    out = jnp.zeros((rows, hpg * hd), F32)
    for h in range(hpg):
        out = jnp.where(lane_head == h, v[:, h:h + 1], out)
    return out


def _ssd_kernel(x_ref, b_ref, c_ref, z_ref, dtt_ref, dtb_ref, alog_ref, dexp_ref, ng_ref,
                o_ref, s_ref, y_ref, *, hpg, hd, q):
    @pl.when(pl.program_id(2) == 0)
    def _():
        s_ref[...] = jnp.zeros_like(s_ref)

    xg = x_ref[...]
    bg = b_ref[...]
    cg = c_ref[...]
    cgb = cg.astype(BF16)
    v = dtt_ref[...] + dtb_ref[...]
    dt = jnp.maximum(v, 0.0) + jnp.log1p(jnp.exp(-jnp.abs(v)))
    a = dt * (-LOG2_E * jnp.exp(alog_ref[...]))

    a_pad = jnp.concatenate([a, jnp.zeros((LANES - hpg, q), F32)], axis=0)
    rr = lax.broadcasted_iota(jnp.int32, (q, q), 0)
    cc = lax.broadcasted_iota(jnp.int32, (q, q), 1)
    causal = rr >= cc
    tri_ls = jnp.where(causal, 1.0, 0.0).astype(BF16)
    tri_sl = jnp.where(rr <= cc, 1.0, 0.0).astype(BF16)
    acs_t = jnp.zeros((LANES, q), F32)
    acs = jnp.zeros((q, LANES), F32)
    for piece in _split3(a_pad):
        acs_t = acs_t + _dot(piece, tri_sl)
        acs = acs + _dot_nt(tri_ls, piece)
    acs_t = acs_t[0:hpg, :]
    wst_t = jnp.exp2(acs_t[:, q - 1:q] - acs_t) * dt

    cb = _dot_nt(cgb, bg.astype(BF16))
    bt = bg.T
    pair = LANES // hd
    lane_y = lax.broadcasted_iota(jnp.int32, (SSD_ROWS, LANES), 1)
    lane_s = lax.broadcasted_iota(jnp.int32, (bt.shape[0], LANES), 1)
    s_old = s_ref[...]
    for t in range(hpg // pair):
        ts = slice(t * LANES, (t + 1) * LANES)
        x_t = xg[:, ts].astype(BF16)
        s_t = None
        for p in range(pair):
            h = t * pair + p
            s_h = _dot((bt * wst_t[h:h + 1, :]).astype(BF16), x_t)
            s_t = s_h if p == 0 else jnp.where(lane_s >= p * hd, s_h, s_t)
        for rc in range(q // SSD_ROWS):
            rs = slice(rc * SSD_ROWS, (rc + 1) * SSD_ROWS)
            kc = min(q, LANES * (-(-(rc + 1) * SSD_ROWS // LANES)))
            causal_c = (lax.broadcasted_iota(jnp.int32, (SSD_ROWS, kc), 0) + rc * SSD_ROWS
                        >= lax.broadcasted_iota(jnp.int32, (SSD_ROWS, kc), 1))
            y_c = None
            for p in range(pair):
                h = t * pair + p
                seg = acs[rs, h:h + 1] - acs_t[h:h + 1, :kc]
                decay = jnp.exp2(jnp.where(causal_c, seg, -jnp.inf))
                m = (cb[rs, :kc] * decay * dt[h:h + 1, :kc]).astype(BF16)
                y_h = _dot(m, x_t[:kc])
                y_c = y_h if p == 0 else jnp.where(lane_y >= p * hd, y_h, y_c)
            y_ref[rs, ts] = y_c
        dec_t = _expand_heads(jnp.exp2(acs[q - 1:q, t * pair:(t + 1) * pair]), pair, hd)
        s_ref[:, ts] = s_old[:, ts] * dec_t + s_t

    y = y_ref[...]
    y = y + _dot(cgb, s_old.astype(BF16)) * _expand_heads(jnp.exp2(acs[:, 0:hpg]), hpg, hd)
    y = y + xg * dexp_ref[...]
    y = y * _silu(z_ref[...])
    y = y * lax.rsqrt(jnp.mean(y * y, axis=-1, keepdims=True) + EPS) * ng_ref[...]
    o_ref[...] = y.astype(o_ref.dtype)


def _ssd(zx, xbc, dtt, dt_bias, a_log, d_exp, norm_g, *, bsz, seq, d_inner, n_state):
    groups = SSM_GROUPS
    heads = dtt.shape[0]
    hpg = heads // groups
    hd = d_inner // heads
    gw = hpg * hd
    q = _blk(seq, SSM_CHUNK)
    nc = seq // q
    assert gw % LANES == 0 and LANES % hd == 0 and n_state % LANES == 0
    assert d_inner // groups == gw
    b_off = d_inner // n_state
    c_off = b_off + groups
    row = lambda b, g, c: b * nc + c
    return pl.pallas_call(
        functools.partial(_ssd_kernel, hpg=hpg, hd=hd, q=q),
        out_shape=jax.ShapeDtypeStruct((bsz * seq, d_inner), BF16),
        grid=(bsz, groups, nc),
        in_specs=[
            pl.BlockSpec((q, gw), lambda b, g, c: (row(b, g, c), g)),
            pl.BlockSpec((q, n_state), lambda b, g, c: (row(b, g, c), b_off + g)),
            pl.BlockSpec((q, n_state), lambda b, g, c: (row(b, g, c), c_off + g)),
            pl.BlockSpec((q, gw), lambda b, g, c: (row(b, g, c), g)),
            pl.BlockSpec((hpg, q), lambda b, g, c: (g, row(b, g, c))),
            pl.BlockSpec((hpg, 1), lambda b, g, c: (g, 0)),
            pl.BlockSpec((hpg, 1), lambda b, g, c: (g, 0)),
            pl.BlockSpec((1, gw), lambda b, g, c: (0, g)),
            pl.BlockSpec((1, gw), lambda b, g, c: (0, g)),
        ],
        out_specs=pl.BlockSpec((q, gw), lambda b, g, c: (row(b, g, c), g)),
        scratch_shapes=[pltpu.VMEM((n_state, gw), F32), pltpu.VMEM((q, gw), F32)],
        compiler_params=_cparams("parallel", "parallel", "arbitrary"),
        name="ssd",
    )(xbc, xbc, xbc, zx, dtt, dt_bias, a_log, d_exp, norm_g)


def _router_kernel(x_ref, g_ref, wr_ref, idx_ref, gw_ref, *, n_experts):
    h = _rms(x_ref[...], g_ref[...])
    w = wr_ref[...]
    hh = h.astype(BF16)
    hl = (h - hh.astype(F32)).astype(BF16)
    wh = w.astype(BF16)
    wl = (w - wh.astype(F32)).astype(BF16)
    logits = _dot(hh, wh) + _dot(hl, wh) + _dot(hh, wl)
    lane = lax.broadcasted_iota(jnp.int32, logits.shape, 1).astype(F32)
    neg = -jnp.inf
    lg = jnp.where(lane < n_experts, logits, neg)
    m1 = jnp.max(lg, axis=-1, keepdims=True)
    i1 = jnp.min(jnp.where(lg == m1, lane, float(LANES)), axis=-1, keepdims=True)
    lg2 = jnp.where(lane == i1, neg, lg)
    m2 = jnp.max(lg2, axis=-1, keepdims=True)
    i2 = jnp.min(jnp.where(lg2 == m2, lane, float(LANES)), axis=-1, keepdims=True)
    e = jnp.exp(m2 - m1)
    w1 = 1.0 / (1.0 + e)
    w2 = e / (1.0 + e)
    idx_ref[...] = jnp.where(lane == 0.0, i1, i2).astype(jnp.int32)
    gw_ref[...] = jnp.where(lane == 0.0, w1, w2)


def _router(x, g, wr, n_experts):
    m, k = x.shape
    bm = _blk(m, 512)
    return pl.pallas_call(
        functools.partial(_router_kernel, n_experts=n_experts),
        out_shape=(jax.ShapeDtypeStruct((m, LANES), jnp.int32),
                   jax.ShapeDtypeStruct((m, LANES), F32)),
        grid=(m // bm,),
        in_specs=[
            pl.BlockSpec((bm, k), lambda i: (i, 0)),
            pl.BlockSpec((1, k), lambda i: (0, 0)),
            pl.BlockSpec((k, LANES), lambda i: (0, 0)),
        ],
        out_specs=(pl.BlockSpec((bm, LANES), lambda i: (i, 0)),
                   pl.BlockSpec((bm, LANES), lambda i: (i, 0))),
        compiler_params=_cparams("parallel"),
        name="router",
    )(x, g, wr)


def _rank_kernel(e_ref, rank_ref, cnt_ref, carry_ref, *, ep, bl):
    @pl.when(pl.program_id(0) == 0)
    def _():
        carry_ref[...] = jnp.zeros_like(carry_ref)

    e_row = e_ref[0]
    sub = lax.broadcasted_iota(jnp.int32, (ep, bl), 0)
    hit = sub == e_row
    oh = jnp.where(hit, 1.0, 0.0).astype(BF16)
    rr = lax.broadcasted_iota(jnp.int32, (bl, bl), 0)
    cc = lax.broadcasted_iota(jnp.int32, (bl, bl), 1)
    tri = jnp.where(rr <= cc, 1.0, 0.0).astype(BF16)
    pre = _dot(oh, tri)
    carry = carry_ref[...]
    rank = jnp.sum(jnp.where(hit, pre - 1.0 + carry[:, 0:1], 0.0), axis=0, keepdims=True)
    rank_ref[0] = rank.astype(jnp.int32)
    carry = carry + pre[:, bl - 1:bl]
    carry_ref[...] = carry
    cnt_ref[...] = carry.astype(jnp.int32)


def _ranks(e_flat, n_experts):
    n = e_flat.shape[0]
    bl = _blk(n, 512)
    ep = 2 * SUBLANES * (-(-n_experts // (2 * SUBLANES)))
    rank, cnt = pl.pallas_call(
        functools.partial(_rank_kernel, ep=ep, bl=bl),
        out_shape=(jax.ShapeDtypeStruct((n // bl, 1, bl), jnp.int32),
                   jax.ShapeDtypeStruct((ep, LANES), jnp.int32)),
        grid=(n // bl,),
        in_specs=[pl.BlockSpec((1, 1, bl), lambda i: (i, 0, 0))],
        out_specs=(pl.BlockSpec((1, 1, bl), lambda i: (i, 0, 0)),
                   pl.BlockSpec((ep, LANES), lambda i: (0, 0))),
        scratch_shapes=[pltpu.VMEM((ep, LANES), F32)],
        compiler_params=_cparams("arbitrary"),
        name="expert_ranks",
    )(e_flat.reshape(n // bl, 1, bl))
    return rank.reshape(n), cnt[:n_experts, 0]


def _row_copy(src, dst, s_row, d_row, sem):
    return pltpu.make_async_copy(src.at[pl.ds(s_row, 1)], dst.at[pl.ds(d_row, 1)], sem)


def _scatter_kernel(pos_ref, x_ref, g_ref, xs_in_ref, xs_ref, h_ref, sem, *, rb, n_tok):
    del xs_in_ref
    base = pl.program_id(0) * rb
    h_ref[...] = _rms(x_ref[...], g_ref[...])

    def issue(r, carry):
        for k in range(TOP_K):
            _row_copy(h_ref, xs_ref, r, pos_ref[k * n_tok + base + r], sem).start()
        return carry

    lax.fori_loop(0, rb, issue, 0, unroll=8)

    def drain(r, carry):
        for k in range(TOP_K):
            _row_copy(h_ref, xs_ref, 0, 0, sem).wait()
        return carry

    lax.fori_loop(0, rb, drain, 0, unroll=8)


def _scatter_rows(pos, x, g, n_rows):
    n_tok, k = x.shape
    rb = _blk(n_tok, 256)
    xs0 = jnp.zeros((n_rows, k), F32)
    return pl.pallas_call(
        functools.partial(_scatter_kernel, rb=rb, n_tok=n_tok),
        out_shape=jax.ShapeDtypeStruct((n_rows, k), F32),
        grid_spec=pltpu.PrefetchScalarGridSpec(
            num_scalar_prefetch=1,
            grid=(n_tok // rb,),
            in_specs=[pl.BlockSpec((rb, k), lambda i, pos: (i, 0)),
                      pl.BlockSpec((1, k), lambda i, pos: (0, 0)),
                      pl.BlockSpec(memory_space=pl.ANY)],
            out_specs=pl.BlockSpec(memory_space=pl.ANY),
            scratch_shapes=[pltpu.VMEM((rb, k), F32), pltpu.SemaphoreType.DMA],
        ),
        input_output_aliases={3: 0},
        compiler_params=_cparams("arbitrary"),
        name="moe_scatter",
    )(pos, x, g, xs0)


N_TILE_TABLES = 6


def _tile_tables(te, nu):
    first = jnp.concatenate([jnp.ones((1,), jnp.int32), (te[1:] != te[:-1]).astype(jnp.int32)])
    run = jnp.cumsum(first) - 1
    later = jnp.where(te[None, :] > te[:, None], te[None, :], jnp.iinfo(jnp.int32).max)
    nxt = jnp.min(later, axis=1)
    nxt = jnp.where(nxt == jnp.iinfo(jnp.int32).max, te[0], nxt)
    return (te, nu, first, run.astype(jnp.int32), nxt.astype(jnp.int32),
            (run[-1:] + 1).astype(jnp.int32))


def _weight_copies(w_refs, wbuf, sems, e, j, slot, bn):
    cols = pl.ds(pl.multiple_of(j * bn, bn), bn)
    return [pltpu.make_async_copy(w.at[e, :, cols], wbuf.at[slot, n], sems.at[slot, n])
            for n, w in enumerate(w_refs)]


def _run_weights(tables, w_refs, wbuf, sems, bn):
    te_ref, _, first_ref, run_ref, nxt_ref, nruns_ref = tables
    j, i = pl.program_id(0), pl.program_id(1)
    n_runs = nruns_ref[0]
    g = j * n_runs + run_ref[i]
    slot = lax.rem(g, 2)

    @pl.when(first_ref[i] == 1)
    def _():
        @pl.when(g == 0)
        def _():
            for c in _weight_copies(w_refs, wbuf, sems, te_ref[0], 0, 0, bn):
                c.start()

        for c in _weight_copies(w_refs, wbuf, sems, te_ref[i], j, slot, bn):
            c.wait()
        last_run = run_ref[i] == n_runs - 1

        @pl.when(jnp.logical_not(jnp.logical_and(last_run, j == pl.num_programs(0) - 1)))
        def _():
            nj = j + last_run.astype(jnp.int32)
            for c in _weight_copies(w_refs, wbuf, sems, nxt_ref[i], nj, 1 - slot, bn):
                c.start()

    return slot


def _moe_up_kernel(*refs, bn):
    tables = refs[:N_TILE_TABLES]
    xs_ref, wg_ref, wu_ref, o_ref, wbuf, sems = refs[N_TILE_TABLES:]
    slot = _run_weights(tables, (wg_ref, wu_ref), wbuf, sems, bn)

    @pl.when(pl.program_id(1) < tables[1][0])
    def _():
        h = xs_ref[...].astype(BF16)
        for c in range(bn // MXU_COLS):
            cs = slice(c * MXU_COLS, (c + 1) * MXU_COLS)
            a = _dot(h, wbuf[slot, 0, :, cs].astype(BF16))
            b = _dot(h, wbuf[slot, 1, :, cs].astype(BF16))
            o_ref[:, cs] = (_silu(a) * b).astype(o_ref.dtype)

    @pl.when(pl.program_id(1) >= tables[1][0])
    def _():
        o_ref[...] = jnp.zeros_like(o_ref)


def _moe_up(tables, xs, wg, wu, bm):
    rows, k = xs.shape
    f = wg.shape[2]
    bn = _blk(f, 1024)
    return pl.pallas_call(
        functools.partial(_moe_up_kernel, bn=bn),
        out_shape=jax.ShapeDtypeStruct((rows, f), BF16),
        grid_spec=pltpu.PrefetchScalarGridSpec(
            num_scalar_prefetch=N_TILE_TABLES,
            grid=(f // bn, rows // bm),
            in_specs=[
                pl.BlockSpec((bm, k), lambda j, i, *_: (i, 0)),
                pl.BlockSpec(memory_space=pl.ANY),
                pl.BlockSpec(memory_space=pl.ANY),
            ],
            out_specs=pl.BlockSpec((bm, bn), lambda j, i, *_: (i, j)),
            scratch_shapes=[pltpu.VMEM((2, 2, k, bn), F32), pltpu.SemaphoreType.DMA((2, 2))],
        ),
        compiler_params=_cparams("arbitrary", "arbitrary"),
        name="moe_up",
    )(*tables, xs, wg, wu)


def _moe_down_kernel(*refs, bn):
    tables = refs[:N_TILE_TABLES]
    a_ref, wd_ref, o_ref, wbuf, sems = refs[N_TILE_TABLES:]
    slot = _run_weights(tables, (wd_ref,), wbuf, sems, bn)

    @pl.when(pl.program_id(1) < tables[1][0])
    def _():
        a = a_ref[...]
        for c in range(bn // MXU_COLS):
            cs = slice(c * MXU_COLS, (c + 1) * MXU_COLS)
            o_ref[:, cs] = _dot(a, wbuf[slot, 0, :, cs].astype(BF16))

    @pl.when(pl.program_id(1) >= tables[1][0])
    def _():
        o_ref[...] = jnp.zeros_like(o_ref)


def _moe_down(tables, act, wd, bm):
    rows, f = act.shape
    n = wd.shape[2]
    bn = _blk(n, 512)
    return pl.pallas_call(
        functools.partial(_moe_down_kernel, bn=bn),
        out_shape=jax.ShapeDtypeStruct((rows, n), F32),
        grid_spec=pltpu.PrefetchScalarGridSpec(
            num_scalar_prefetch=N_TILE_TABLES,
            grid=(n // bn, rows // bm),
            in_specs=[
                pl.BlockSpec((bm, f), lambda j, i, *_: (i, 0)),
                pl.BlockSpec(memory_space=pl.ANY),
            ],
            out_specs=pl.BlockSpec((bm, bn), lambda j, i, *_: (i, j)),
            scratch_shapes=[pltpu.VMEM((2, 1, f, bn), F32), pltpu.SemaphoreType.DMA((2, 1))],
        ),
        compiler_params=_cparams("arbitrary", "arbitrary"),
        name="moe_down",
    )(*tables, act, wd)


def _combine_kernel(pos_ref, x_ref, gw_ref, fg_ref, ys_ref, o_ref, buf_ref, sems, *, rb, n_tok):
    i = pl.program_id(0)
    slot = lax.rem(i, 2)

    def gather(block, to_slot):
        def issue(r, carry):
            for k in range(TOP_K):
                _row_copy(ys_ref, buf_ref.at[to_slot, k], pos_ref[k * n_tok + block * rb + r], r,
                          sems.at[to_slot]).start()
            return carry

        lax.fori_loop(0, rb, issue, 0, unroll=8)

    @pl.when(i == 0)
    def _():
        gather(0, 0)

    @pl.when(i + 1 < pl.num_programs(0))
    def _():
        gather(i + 1, 1 - slot)

    def drain(r, carry):
        for k in range(TOP_K):
            _row_copy(ys_ref, buf_ref.at[slot, k], 0, 0, sems.at[slot]).wait()
        return carry

    lax.fori_loop(0, rb, drain, 0, unroll=8)

    gw = gw_ref[...]
    y = x_ref[...] + gw[:, 0:1] * buf_ref[slot, 0] + gw[:, 1:2] * buf_ref[slot, 1]
    o_ref[...] = _rms(y, fg_ref[...])


def _combine_norm(pos, x, gw, fg, ys):
    n_tok, d = x.shape
    rb = _blk(n_tok, 256)
    return pl.pallas_call(
        functools.partial(_combine_kernel, rb=rb, n_tok=n_tok),
        out_shape=jax.ShapeDtypeStruct((n_tok, d), F32),
        grid_spec=pltpu.PrefetchScalarGridSpec(
            num_scalar_prefetch=1,
            grid=(n_tok // rb,),
            in_specs=[
                pl.BlockSpec((rb, d), lambda i, pos: (i, 0)),
                pl.BlockSpec((rb, LANES), lambda i, pos: (i, 0)),
                pl.BlockSpec((1, d), lambda i, pos: (0, 0)),
                pl.BlockSpec(memory_space=pl.ANY),
            ],
            out_specs=pl.BlockSpec((rb, d), lambda i, pos: (i, 0)),
            scratch_shapes=[pltpu.VMEM((2, TOP_K, rb, d), F32), pltpu.SemaphoreType.DMA((2,))],
        ),
        compiler_params=_cparams("arbitrary"),
        name="moe_combine_norm",
    )(pos, x, gw, fg, ys)


def _moe_block(x, norm_g, final_g, w_router, w_gate, w_up, w_down):
    n_tok, d = x.shape
    n_experts = w_router.shape[1]
    bm = _blk(n_tok, 512)
    wr = jnp.zeros((d, LANES), F32).at[:, :n_experts].set(w_router)
    idx, gw = _router(x, norm_g, wr, n_experts)

    e_flat = idx[:, :TOP_K].T.reshape(TOP_K * n_tok)
    rank, counts = _ranks(e_flat, n_experts)

    tiles = (counts + bm - 1) // bm
    tile_end = jnp.cumsum(tiles)
    row_off = (tile_end - tiles) * bm
    pos = row_off[e_flat] + rank
    n_tiles = TOP_K * n_tok // bm + n_experts
    n_used = tile_end[-1]
    tile_ids = jnp.minimum(jnp.arange(n_tiles, dtype=jnp.int32), n_used - 1)
    te = jnp.sum(tile_ids[:, None] >= tile_end[None, :], axis=1).astype(jnp.int32)
    nu = n_used.reshape(1).astype(jnp.int32)

    xs = _scatter_rows(pos, x, norm_g, n_tiles * bm)
    tables = _tile_tables(te, nu)
    act = _moe_up(tables, xs, w_gate, w_up, bm)
    ys = _moe_down(tables, act, w_down, bm)
    return _combine_norm(pos, x, gw, final_g, ys)


def _conformer_layer(x, bsz, seq, mix_g, ffn_g, pw1_w, pw1_b, dw_w, dw_b, ln_g, ln_b,
                     pw2_w, pw2_b, w_gate, w_up, w_down):
    d = x.shape[1]
    row = lambda v: v.reshape(1, -1)
    glu = _norm_glu(x, row(mix_g), pw1_w.astype(BF16), row(pw1_b), F32)
    hc = _dwconv(glu.reshape(bsz, seq, d), dw_w, row(dw_b), (row(ln_g), row(ln_b)), BF16)
    x = _mm_res(hc.reshape(bsz * seq, d), pw2_w.astype(BF16), row(pw2_b), x)
    act = _norm_swiglu(x, row(ffn_g), w_gate.astype(BF16), w_up.astype(BF16))
    return _mm_res(act, w_down.astype(BF16), jnp.zeros((1, d), F32), x)


def _mamba_mixer(x, bsz, seq, mix_g, w_in, conv_w, conv_b, dt_bias, a_log, d_skip, norm_g, w_out):
    d = x.shape[1]
    heads = dt_bias.shape[0]
    d_inner = norm_g.shape[0]
    conv_dim = conv_w.shape[1]
    n_state = (conv_dim - d_inner) // (2 * SSM_GROUPS)
    hd = d_inner // heads
    row = lambda v: v.reshape(1, -1)
    n_main = d_inner + conv_dim
    w_dt = jnp.zeros((d, LANES), BF16).at[:, :heads].set(w_in[:, n_main:].astype(BF16))
    zx, dt_raw = _norm_inproj(x, row(mix_g), w_in.astype(BF16), w_dt, n_main)
    xbc = _dwconv(zx.reshape(bsz, seq, n_main), conv_w, row(conv_b), None, F32, c_start=d_inner)
    y = _ssd(zx, xbc.reshape(bsz * seq, conv_dim), dt_raw[:, :heads].T,
             dt_bias.reshape(heads, 1), a_log.reshape(heads, 1),
             row(jnp.repeat(d_skip, hd)), row(norm_g),
             bsz=bsz, seq=seq, d_inner=d_inner, n_state=n_state)
    return _mm_res(y, w_out.astype(BF16), jnp.zeros((1, d), F32), x)


def kernel(x, norm_mix_g, norm_ffn_g, final_norm_g, conf_pw1_w, conf_pw1_b, conf_dw_w, conf_dw_b, conf_ln_g, conf_ln_b, conf_pw2_w, conf_pw2_b, ssm_in_w, ssm_conv_w, ssm_conv_b, ssm_dt_bias, ssm_a_log, ssm_d, ssm_norm_g, ssm_out_w, ffn_w_gate, ffn_w_up, ffn_w_down, moe_router_w, moe_w_gate, moe_w_up, moe_w_down):
    bsz, seq, d = x.shape
    assert norm_mix_g.shape[0] == 2, "two layers: Conformer conv + SwiGLU, then Mamba-2 + MoE"
    xf = x.reshape(bsz * seq, d)
    xf = _conformer_layer(xf, bsz, seq, norm_mix_g[0], norm_ffn_g[0], conf_pw1_w[0], conf_pw1_b[0],
                          conf_dw_w[0], conf_dw_b[0], conf_ln_g[0], conf_ln_b[0], conf_pw2_w[0],
                          conf_pw2_b[0], ffn_w_gate[0], ffn_w_up[0], ffn_w_down[0])
    xf = _mamba_mixer(xf, bsz, seq, norm_mix_g[1], ssm_in_w[0], ssm_conv_w[0], ssm_conv_b[0],
                      ssm_dt_bias[0], ssm_a_log[0], ssm_d[0], ssm_norm_g[0], ssm_out_w[0])
    out = _moe_block(xf, norm_ffn_g[1].reshape(1, d), final_norm_g.reshape(1, d),
                     moe_router_w[0], moe_w_gate.reshape(moe_w_gate.shape[1:]),
                     moe_w_up.reshape(moe_w_up.shape[1:]), moe_w_down.reshape(moe_w_down.shape[1:]))
    return out.reshape(bsz, seq, d)
```

```python
import functools

import jax
import jax.numpy as jnp
from jax import lax
from jax.experimental import pallas as pl
from jax.experimental.pallas import tpu as pltpu

EPS = 1e-6
SSM_GROUPS = 8
SSM_CHUNK = 256
TOP_K = 2
LANES = 128
SUBLANES = 8
MXU_COLS = 256
LOG2_E = 1.4426950408889634
VMEM_LIMIT = 56 << 20
RESIDENT_WEIGHT_BYTES = 8 << 20

F32 = jnp.float32
BF16 = jnp.bfloat16


def _cparams(*sem):
    return pltpu.CompilerParams(dimension_semantics=sem, vmem_limit_bytes=VMEM_LIMIT)


def _sigmoid(v):
    return 1.0 / (1.0 + jnp.exp(-v))


def _silu(v):
    return v * _sigmoid(v)


def _rms(x, g):
    return x * lax.rsqrt(jnp.mean(x * x, axis=-1, keepdims=True) + EPS) * g


def _dot(a, b):
    return jnp.dot(a, b, preferred_element_type=F32)


def _dot_nt(a, b):
    return lax.dot_general(a, b, (((1,), (1,)), ((), ())), preferred_element_type=F32)


def _blk(n, want):
    b = min(n, want)
    assert n % b == 0, (n, want)
    return b


def _norm_glu_kernel(x_ref, g_ref, wa_ref, wb_ref, ba_ref, bb_ref, o_ref, h_ref):
    @pl.when(pl.program_id(1) == 0)
    def _():
        h_ref[...] = _rms(x_ref[...], g_ref[...]).astype(BF16)

    h = h_ref[...]
    a = _dot(h, wa_ref[...]) + ba_ref[...]
    b = _dot(h, wb_ref[...]) + bb_ref[...]
    o_ref[...] = (a * _sigmoid(b)).astype(o_ref.dtype)


def _norm_glu(x, g, w, b, out_dtype):
    m, k = x.shape
    n = w.shape[1] // 2
    bm, bn = _blk(m, 1024), _blk(n, 1024)
    nb = n // bn
    return pl.pallas_call(
        _norm_glu_kernel,
        out_shape=jax.ShapeDtypeStruct((m, n), out_dtype),
        grid=(m // bm, nb),
        in_specs=[
            pl.BlockSpec((bm, k), lambda i, j: (i, 0)),
            pl.BlockSpec((1, k), lambda i, j: (0, 0)),
            pl.BlockSpec((k, bn), lambda i, j: (0, j)),
            pl.BlockSpec((k, bn), lambda i, j: (0, j + nb)),
            pl.BlockSpec((1, bn), lambda i, j: (0, j)),
            pl.BlockSpec((1, bn), lambda i, j: (0, j + nb)),
        ],
        out_specs=pl.BlockSpec((bm, bn), lambda i, j: (i, j)),
        scratch_shapes=[pltpu.VMEM((bm, k), BF16)],
        compiler_params=_cparams("parallel", "arbitrary"),
        name="norm_glu",
    )(x, g, w, w, b, b)


def _norm_swiglu_kernel(x_ref, g_ref, wg_ref, wu_ref, o_ref, h_ref):
    @pl.when(pl.program_id(1) == 0)
    def _():
        h_ref[...] = _rms(x_ref[...], g_ref[...]).astype(BF16)

    h = h_ref[...]
    a = _dot(h, wg_ref[...])
    b = _dot(h, wu_ref[...])
    o_ref[...] = (_silu(a) * b).astype(o_ref.dtype)


def _norm_swiglu(x, g, wg, wu):
    m, k = x.shape
    n = wg.shape[1]
    bm, bn = _blk(m, 1024), _blk(n, 1024)
    return pl.pallas_call(
        _norm_swiglu_kernel,
        out_shape=jax.ShapeDtypeStruct((m, n), BF16),
        grid=(m // bm, n // bn),
        in_specs=[
            pl.BlockSpec((bm, k), lambda i, j: (i, 0)),
            pl.BlockSpec((1, k), lambda i, j: (0, 0)),
            pl.BlockSpec((k, bn), lambda i, j: (0, j)),
            pl.BlockSpec((k, bn), lambda i, j: (0, j)),
        ],
        out_specs=pl.BlockSpec((bm, bn), lambda i, j: (i, j)),
        scratch_shapes=[pltpu.VMEM((bm, k), BF16)],
        compiler_params=_cparams("parallel", "arbitrary"),
        name="norm_swiglu",
    )(x, g, wg, wu)


def _norm_inproj_kernel(x_ref, g_ref, w_ref, wdt_ref, o_ref, dt_ref, h_ref):
    @pl.when(pl.program_id(1) == 0)
    def _():
        h = _rms(x_ref[...], g_ref[...]).astype(BF16)
        h_ref[...] = h
        dt_ref[...] = _dot(h, wdt_ref[...])

    o_ref[...] = _dot(h_ref[...], w_ref[...])


def _norm_inproj(x, g, w, wdt, n):
    m, k = x.shape
    bm, bn = _blk(m, 1024), _blk(n, 1024)
    return pl.pallas_call(
        _norm_inproj_kernel,
        out_shape=(jax.ShapeDtypeStruct((m, n), F32),
                   jax.ShapeDtypeStruct((m, wdt.shape[1]), F32)),
        grid=(m // bm, n // bn),
        in_specs=[
            pl.BlockSpec((bm, k), lambda i, j: (i, 0)),
            pl.BlockSpec((1, k), lambda i, j: (0, 0)),
            pl.BlockSpec((k, bn), lambda i, j: (0, j)),
            pl.BlockSpec((k, wdt.shape[1]), lambda i, j: (0, 0)),
        ],
        out_specs=(pl.BlockSpec((bm, bn), lambda i, j: (i, j)),
                   pl.BlockSpec((bm, wdt.shape[1]), lambda i, j: (i, 0))),
        scratch_shapes=[pltpu.VMEM((bm, k), BF16)],
        compiler_params=_cparams("parallel", "arbitrary"),
        name="norm_inproj",
    )(x, g, w, wdt)


def _mm_res_kernel(a_ref, w_ref, b_ref, r_ref, o_ref):
    o_ref[...] = r_ref[...] + _dot(a_ref[...], w_ref[...]) + b_ref[...]


def _mm_res(a, w, bias, res):
    m, k = a.shape
    n = w.shape[1]
    if k * n * 2 <= RESIDENT_WEIGHT_BYTES:
        bm, bn = _blk(m, 512), n
    else:
        bm, bn = _blk(m, 1024), _blk(n, 256 if k > 4096 else 512)
    return pl.pallas_call(
        _mm_res_kernel,
        out_shape=jax.ShapeDtypeStruct((m, n), F32),
        grid=(m // bm, n // bn),
        in_specs=[
            pl.BlockSpec((bm, k), lambda i, j: (i, 0)),
            pl.BlockSpec((k, bn), lambda i, j: (0, j)),
            pl.BlockSpec((1, bn), lambda i, j: (0, j)),
            pl.BlockSpec((bm, bn), lambda i, j: (i, j)),
        ],
        out_specs=pl.BlockSpec((bm, bn), lambda i, j: (i, j)),
        compiler_params=_cparams("parallel", "arbitrary"),
        name="mm_res",
    )(a, w, bias, res)


def _conv_rows(taps):
    return 16 if taps > SUBLANES else 32


def _dwconv_kernel(*refs, taps, halo, bt, cb, layer_norm):
    rows = _conv_rows(taps)
    if layer_norm:
        xc_ref, xh_ref, w_ref, b_ref, lg_ref, lb_ref, o_ref, win_ref, y_ref = refs
    else:
        xc_ref, xh_ref, w_ref, b_ref, o_ref, win_ref = refs
        y_ref = None
    first = pl.program_id(1) == 0
    win_ref[0, 0:halo, :] = jnp.where(first, 0.0, xh_ref[...])
    win_ref[0, halo:halo + bt, :] = xc_ref[...]
    off = halo - (taps - 1)
    shift_in_regs = taps <= SUBLANES
    n_sh = halo + bt - SUBLANES
    if not shift_in_regs:
        for s in sorted({(off + j) % SUBLANES for j in range(taps)} - {0}):
            win_ref[s, 0:n_sh, :] = win_ref[0, s:s + n_sh, :]

    def body(r, carry):
        r0 = pl.multiple_of(r * rows, rows)
        for c in range(cb // LANES):
            cs = slice(c * LANES, (c + 1) * LANES)
            acc = jnp.broadcast_to(b_ref[:, cs], (rows, LANES))
            if shift_in_regs:
                piece = win_ref[0, pl.ds(r0, rows + halo), cs]
            for j in range(taps):
                if shift_in_regs:
                    xw = piece[off + j:off + j + rows]
                else:
                    al, s = divmod(off + j, SUBLANES)
                    xw = win_ref[s, pl.ds(r0 + al * SUBLANES, rows), cs]
                acc = acc + xw * w_ref[j:j + 1, cs]
            if layer_norm:
                y_ref[pl.ds(r0, rows), cs] = acc
            else:
                o_ref[pl.ds(r0, rows), cs] = _silu(acc).astype(o_ref.dtype)
        return carry

    lax.fori_loop(0, bt // rows, body, 0)

    if layer_norm:
        y = y_ref[...]
        mu = jnp.mean(y, axis=-1, keepdims=True)
        d = y - mu
        var = jnp.mean(d * d, axis=-1, keepdims=True)
        hn = d * lax.rsqrt(var + EPS) * lg_ref[...] + lb_ref[...]
        o_ref[...] = _silu(hn).astype(o_ref.dtype)


def _dwconv(x3, w, b, ln, out_dtype, c_start=0):
    bsz, seq, _ = x3.shape
    taps, ch = w.shape
    halo = SUBLANES * (-(-(taps - 1) // SUBLANES))
    bt = _blk(seq, 256)
    assert bt % halo == 0 and bt % _conv_rows(taps) == 0
    layer_norm = ln is not None
    cb = ch if layer_norm else _blk(ch, 1024)
    assert c_start % cb == 0
    co = c_start // cb
    hb = bt // halo
    in_specs = [
        pl.BlockSpec((None, bt, cb), lambda bi, i, c: (bi, i, c + co)),
        pl.BlockSpec((None, halo, cb),
                     lambda bi, i, c: (bi, jnp.maximum(i * hb - 1, 0), c + co)),
        pl.BlockSpec((taps, cb), lambda bi, i, c: (0, c)),
        pl.BlockSpec((1, cb), lambda bi, i, c: (0, c)),
    ]
    args = [x3, x3, w, b]
    scratch = [pltpu.VMEM((SUBLANES if taps > SUBLANES else 1, halo + bt, cb), F32)]
    if layer_norm:
        in_specs += [pl.BlockSpec((1, cb), lambda bi, i, c: (0, c))] * 2
        args += [ln[0], ln[1]]
        scratch.append(pltpu.VMEM((bt, cb), F32))
    return pl.pallas_call(
        functools.partial(_dwconv_kernel, taps=taps, halo=halo, bt=bt, cb=cb,
                          layer_norm=layer_norm),
        out_shape=jax.ShapeDtypeStruct((bsz, seq, ch), out_dtype),
        grid=(bsz, seq // bt, ch // cb),
        in_specs=in_specs,
        out_specs=pl.BlockSpec((None, bt, cb), lambda bi, i, c: (bi, i, c)),
        scratch_shapes=scratch,
        compiler_params=_cparams("parallel", "parallel", "parallel"),
        name="dwconv_ln" if layer_norm else "dwconv",
    )(*args)


SSD_ROWS = 64


def _split3(v):
    hi = v.astype(BF16)
    r = v - hi.astype(F32)
    mid = r.astype(BF16)
    lo = (r - mid.astype(F32)).astype(BF16)
    return hi, mid, lo


def _expand_heads(v, hpg, hd):
    rows = v.shape[0]
    lane_head = lax.broadcasted_iota(jnp.int32, (rows, hpg * hd), 1) // hd
    out = jnp.zeros((rows, hpg * hd), F32)
    for h in range(hpg):
        out = jnp.where(lane_head == h, v[:, h:h + 1], out)
    return out


def _ssd_decay_kernel(dtt_ref, dtb_ref, alog_ref, dt_ref, acst_ref, acs_ref, *, hpg, q):
    v = dtt_ref[...] + dtb_ref[...]
    dt = jnp.maximum(v, 0.0) + jnp.log1p(jnp.exp(-jnp.abs(v)))
    a = dt * (-LOG2_E * jnp.exp(alog_ref[...]))
    dt_ref[...] = dt

    rr = lax.broadcasted_iota(jnp.int32, (q, q), 0)
    cc = lax.broadcasted_iota(jnp.int32, (q, q), 1)
    tri_ls = jnp.where(rr >= cc, 1.0, 0.0).astype(BF16)
    tri_sl = jnp.where(rr <= cc, 1.0, 0.0).astype(BF16)
    acs_t = jnp.zeros(a.shape, F32)
    for piece in _split3(a):
        acs_t = acs_t + _dot(piece, tri_sl)
    acst_ref[...] = acs_t
    heads = a.shape[0]
    a_pad = jnp.concatenate([a, jnp.zeros((LANES - heads, q), F32)], axis=0) if heads < LANES else a
    acs = jnp.zeros((q, LANES), F32)
    for piece in _split3(a_pad):
        acs = acs + _dot_nt(tri_ls, piece)
    for g in range(heads // hpg):
        rolled = acs if g == 0 else pltpu.roll(acs, LANES - g * hpg, axis=1)
        acs_ref[:, g * LANES:(g + 1) * LANES] = rolled


def _ssd_decay(dtt, dt_bias, a_log, *, seq):
    heads, n_tok = dtt.shape
    hpg = heads // SSM_GROUPS
    q = _blk(seq, SSM_CHUNK)
    return pl.pallas_call(
        functools.partial(_ssd_decay_kernel, hpg=hpg, q=q),
        out_shape=(jax.ShapeDtypeStruct((heads, n_tok), F32),
                   jax.ShapeDtypeStruct((heads, n_tok), F32),
                   jax.ShapeDtypeStruct((n_tok, SSM_GROUPS * LANES), F32)),
        grid=(n_tok // q,),
        in_specs=[
            pl.BlockSpec((heads, q), lambda r: (0, r)),
            pl.BlockSpec((heads, 1), lambda r: (0, 0)),
            pl.BlockSpec((heads, 1), lambda r: (0, 0)),
        ],
        out_specs=(pl.BlockSpec((heads, q), lambda r: (0, r)),
                   pl.BlockSpec((heads, q), lambda r: (0, r)),
                   pl.BlockSpec((q, SSM_GROUPS * LANES), lambda r: (r, 0))),
        compiler_params=_cparams("parallel"),
        name="ssd_decay",
    )(dtt, dt_bias, a_log)


def _ssd_kernel(x_ref, b_ref, c_ref, z_ref, dt_ref, acst_ref, acs_ref, dexp_ref, ng_ref,
                o_ref, s_ref, y_ref, *, hpg, hd, q):
    @pl.when(pl.program_id(2) == 0)
    def _():
        s_ref[...] = jnp.zeros_like(s_ref)

    xg = x_ref[...]
    bg = b_ref[...]
    cgb = c_ref[...].astype(BF16)
    dt = dt_ref[...]
    acs_t = acst_ref[...]
    acs = acs_ref[...]
    wst_t = jnp.exp2(acs_t[:, q - 1:q] - acs_t) * dt

    cb = _dot_nt(cgb, bg.astype(BF16))
    bt = bg.T
    pair = LANES // hd
    lane_y = lax.broadcasted_iota(jnp.int32, (SSD_ROWS, LANES), 1)
    lane_s = lax.broadcasted_iota(jnp.int32, (bt.shape[0], LANES), 1)
    s_old = s_ref[...]
    for t in range(hpg // pair):
        ts = slice(t * LANES, (t + 1) * LANES)
        x_t = xg[:, ts].astype(BF16)
        s_t = None
        for p in range(pair):
            h = t * pair + p
            s_h = _dot((bt * wst_t[h:h + 1, :]).astype(BF16), x_t)
            s_t = s_h if p == 0 else jnp.where(lane_s >= p * hd, s_h, s_t)
        for rc in range(q // SSD_ROWS):
            rs = slice(rc * SSD_ROWS, (rc + 1) * SSD_ROWS)
            kc = min(q, LANES * (-(-(rc + 1) * SSD_ROWS // LANES)))
            causal_c = (lax.broadcasted_iota(jnp.int32, (SSD_ROWS, kc), 0) + rc * SSD_ROWS
                        >= lax.broadcasted_iota(jnp.int32, (SSD_ROWS, kc), 1))
            y_c = None
            for p in range(pair):
                h = t * pair + p
                seg = acs[rs, h:h + 1] - acs_t[h:h + 1, :kc]
                decay = jnp.exp2(jnp.where(causal_c, seg, -jnp.inf))
                m = (cb[rs, :kc] * decay * dt[h:h + 1, :kc]).astype(BF16)
                y_h = _dot(m, x_t[:kc])
                y_c = y_h if p == 0 else jnp.where(lane_y >= p * hd, y_h, y_c)
            y_ref[rs, ts] = y_c
        dec_t = _expand_heads(jnp.exp2(acs[q - 1:q, t * pair:(t + 1) * pair]), pair, hd)
        s_ref[:, ts] = s_old[:, ts] * dec_t + s_t

    y = y_ref[...]
    y = y + _dot(cgb, s_old.astype(BF16)) * _expand_heads(jnp.exp2(acs[:, 0:hpg]), hpg, hd)
    y = y + xg * dexp_ref[...]
    y = y * _silu(z_ref[...])
    y = y * lax.rsqrt(jnp.mean(y * y, axis=-1, keepdims=True) + EPS) * ng_ref[...]
    o_ref[...] = y.astype(o_ref.dtype)


def _ssd(zx, xbc, dt, acs_t, acs, d_exp, norm_g, *, bsz, seq, d_inner, n_state):
    groups = SSM_GROUPS
    heads = dt.shape[0]
    hpg = heads // groups
    hd = d_inner // heads
    gw = hpg * hd
    q = _blk(seq, SSM_CHUNK)
    nc = seq // q
    assert gw % LANES == 0 and LANES % hd == 0 and n_state % LANES == 0
    assert d_inner // groups == gw
    b_off = d_inner // n_state
    c_off = b_off + groups
    row = lambda b, g, c: b * nc + c
    return pl.pallas_call(
        functools.partial(_ssd_kernel, hpg=hpg, hd=hd, q=q),
        out_shape=jax.ShapeDtypeStruct((bsz * seq, d_inner), BF16),
        grid=(bsz, groups, nc),
        in_specs=[
            pl.BlockSpec((q, gw), lambda b, g, c: (row(b, g, c), g)),
            pl.BlockSpec((q, n_state), lambda b, g, c: (row(b, g, c), b_off + g)),
            pl.BlockSpec((q, n_state), lambda b, g, c: (row(b, g, c), c_off + g)),
            pl.BlockSpec((q, gw), lambda b, g, c: (row(b, g, c), g)),
            pl.BlockSpec((hpg, q), lambda b, g, c: (g, row(b, g, c))),
            pl.BlockSpec((hpg, q), lambda b, g, c: (g, row(b, g, c))),
            pl.BlockSpec((q, LANES), lambda b, g, c: (row(b, g, c), g)),
            pl.BlockSpec((1, gw), lambda b, g, c: (0, g)),
            pl.BlockSpec((1, gw), lambda b, g, c: (0, g)),
        ],
        out_specs=pl.BlockSpec((q, gw), lambda b, g, c: (row(b, g, c), g)),
        scratch_shapes=[pltpu.VMEM((n_state, gw), F32), pltpu.VMEM((q, gw), F32)],
        compiler_params=_cparams("parallel", "parallel", "arbitrary"),
        name="ssd",
    )(xbc, xbc, xbc, zx, dt, acs_t, acs, d_exp, norm_g)


def _router_kernel(x_ref, g_ref, wr_ref, idx_ref, gw_ref, *, n_experts):
    h = _rms(x_ref[...], g_ref[...])
    w = wr_ref[...]
    hh = h.astype(BF16)
    hl = (h - hh.astype(F32)).astype(BF16)
    wh = w.astype(BF16)
    wl = (w - wh.astype(F32)).astype(BF16)
    logits = _dot(hh, wh) + _dot(hl, wh) + _dot(hh, wl)
    lane = lax.broadcasted_iota(jnp.int32, logits.shape, 1).astype(F32)
    neg = -jnp.inf
    lg = jnp.where(lane < n_experts, logits, neg)
    m1 = jnp.max(lg, axis=-1, keepdims=True)
    i1 = jnp.min(jnp.where(lg == m1, lane, float(LANES)), axis=-1, keepdims=True)
    lg2 = jnp.where(lane == i1, neg, lg)
    m2 = jnp.max(lg2, axis=-1, keepdims=True)
    i2 = jnp.min(jnp.where(lg2 == m2, lane, float(LANES)), axis=-1, keepdims=True)
    e = jnp.exp(m2 - m1)
    w1 = 1.0 / (1.0 + e)
    w2 = e / (1.0 + e)
    idx_ref[...] = jnp.where(lane == 0.0, i1, i2).astype(jnp.int32)
    gw_ref[...] = jnp.where(lane == 0.0, w1, w2)


def _router(x, g, wr, n_experts):
    m, k = x.shape
    bm = _blk(m, 512)
    return pl.pallas_call(
        functools.partial(_router_kernel, n_experts=n_experts),
        out_shape=(jax.ShapeDtypeStruct((m, LANES), jnp.int32),
                   jax.ShapeDtypeStruct((m, LANES), F32)),
        grid=(m // bm,),
        in_specs=[
            pl.BlockSpec((bm, k), lambda i: (i, 0)),
            pl.BlockSpec((1, k), lambda i: (0, 0)),
            pl.BlockSpec((k, LANES), lambda i: (0, 0)),
        ],
        out_specs=(pl.BlockSpec((bm, LANES), lambda i: (i, 0)),
                   pl.BlockSpec((bm, LANES), lambda i: (i, 0))),
        compiler_params=_cparams("parallel"),
        name="router",
    )(x, g, wr)


def _rank_kernel(e_ref, rank_ref, cnt_ref, carry_ref, *, ep, bl):
    @pl.when(pl.program_id(0) == 0)
    def _():
        carry_ref[...] = jnp.zeros_like(carry_ref)

    e_row = e_ref[0]
    sub = lax.broadcasted_iota(jnp.int32, (ep, bl), 0)
    hit = sub == e_row
    oh = jnp.where(hit, 1.0, 0.0).astype(BF16)
    rr = lax.broadcasted_iota(jnp.int32, (bl, bl), 0)
    cc = lax.broadcasted_iota(jnp.int32, (bl, bl), 1)
    tri = jnp.where(rr <= cc, 1.0, 0.0).astype(BF16)
    pre = _dot(oh, tri)
    carry = carry_ref[...]
    rank = jnp.sum(jnp.where(hit, pre - 1.0 + carry[:, 0:1], 0.0), axis=0, keepdims=True)
    rank_ref[0] = rank.astype(jnp.int32)
    carry = carry + pre[:, bl - 1:bl]
    carry_ref[...] = carry
    cnt_ref[...] = carry.astype(jnp.int32)


def _ranks(e_flat, n_experts):
    n = e_flat.shape[0]
    bl = _blk(n, 512)
    ep = 2 * SUBLANES * (-(-n_experts // (2 * SUBLANES)))
    rank, cnt = pl.pallas_call(
        functools.partial(_rank_kernel, ep=ep, bl=bl),
        out_shape=(jax.ShapeDtypeStruct((n // bl, 1, bl), jnp.int32),
                   jax.ShapeDtypeStruct((ep, LANES), jnp.int32)),
        grid=(n // bl,),
        in_specs=[pl.BlockSpec((1, 1, bl), lambda i: (i, 0, 0))],
        out_specs=(pl.BlockSpec((1, 1, bl), lambda i: (i, 0, 0)),
                   pl.BlockSpec((ep, LANES), lambda i: (0, 0))),
        scratch_shapes=[pltpu.VMEM((ep, LANES), F32)],
        compiler_params=_cparams("arbitrary"),
        name="expert_ranks",
    )(e_flat.reshape(n // bl, 1, bl))
    return rank.reshape(n), cnt[:n_experts, 0]


def _row_copy(src, dst, s_row, d_row, sem):
    return pltpu.make_async_copy(src.at[pl.ds(s_row, 1)], dst.at[pl.ds(d_row, 1)], sem)


def _scatter_kernel(pos_ref, gap0_ref, gapn_ref, x_ref, g_ref, xs_ref, h_ref, z_ref, sem, zsem,
                    *, rb, n_tok, n_gaps):
    base = pl.program_id(0) * rb

    @pl.when(pl.program_id(0) == 0)
    def _():
        z_ref[...] = jnp.zeros_like(z_ref)
        for e in range(n_gaps):
            def fill(r, carry, e=e):
                _row_copy(z_ref, xs_ref, 0, gap0_ref[e] + r, zsem).start()
                return carry

            lax.fori_loop(0, gapn_ref[e], fill, 0)
        for e in range(n_gaps):
            def fill_done(r, carry):
                _row_copy(z_ref, xs_ref, 0, 0, zsem).wait()
                return carry

            lax.fori_loop(0, gapn_ref[e], fill_done, 0)

    h_ref[...] = _rms(x_ref[...], g_ref[...])

    def issue(r, carry):
        for k in range(TOP_K):
            _row_copy(h_ref, xs_ref, r, pos_ref[k * n_tok + base + r], sem).start()
        return carry

    lax.fori_loop(0, rb, issue, 0, unroll=8)

    def drain(r, carry):
        for k in range(TOP_K):
            _row_copy(h_ref, xs_ref, 0, 0, sem).wait()
        return carry

    lax.fori_loop(0, rb, drain, 0, unroll=8)


def _scatter_rows(pos, gap_start, gap_len, x, g, n_rows):
    n_tok, k = x.shape
    rb = _blk(n_tok, 256)
    return pl.pallas_call(
        functools.partial(_scatter_kernel, rb=rb, n_tok=n_tok, n_gaps=gap_start.shape[0]),
        out_shape=jax.ShapeDtypeStruct((n_rows, k), F32),
        grid_spec=pltpu.PrefetchScalarGridSpec(
            num_scalar_prefetch=3,
            grid=(n_tok // rb,),
            in_specs=[pl.BlockSpec((rb, k), lambda i, *_: (i, 0)),
                      pl.BlockSpec((1, k), lambda i, *_: (0, 0))],
            out_specs=pl.BlockSpec(memory_space=pl.ANY),
            scratch_shapes=[pltpu.VMEM((rb, k), F32), pltpu.VMEM((SUBLANES, k), F32),
                            pltpu.SemaphoreType.DMA, pltpu.SemaphoreType.DMA],
        ),
        compiler_params=_cparams("arbitrary"),
        name="moe_scatter",
    )(pos, gap_start, gap_len, x, g)


N_TILE_TABLES = 6


def _tile_tables(te, nu):
    first = jnp.concatenate([jnp.ones((1,), jnp.int32), (te[1:] != te[:-1]).astype(jnp.int32)])
    run = jnp.cumsum(first) - 1
    later = jnp.where(te[None, :] > te[:, None], te[None, :], jnp.iinfo(jnp.int32).max)
    nxt = jnp.min(later, axis=1)
    nxt = jnp.where(nxt == jnp.iinfo(jnp.int32).max, te[0], nxt)
    return (te, nu, first, run.astype(jnp.int32), nxt.astype(jnp.int32),
            (run[-1:] + 1).astype(jnp.int32))


def _weight_copies(w_refs, wbuf, sems, e, j, slot, bn):
    cols = pl.ds(pl.multiple_of(j * bn, bn), bn)
    return [pltpu.make_async_copy(w.at[e, :, cols], wbuf.at[slot, n], sems.at[slot, n])
            for n, w in enumerate(w_refs)]


def _run_weights(tables, w_refs, wbuf, sems, bn):
    te_ref, _, first_ref, run_ref, nxt_ref, nruns_ref = tables
    j, i = pl.program_id(0), pl.program_id(1)
    n_runs = nruns_ref[0]
    g = j * n_runs + run_ref[i]
    slot = lax.rem(g, 2)

    @pl.when(first_ref[i] == 1)
    def _():
        @pl.when(g == 0)
        def _():
            for c in _weight_copies(w_refs, wbuf, sems, te_ref[0], 0, 0, bn):
                c.start()

        for c in _weight_copies(w_refs, wbuf, sems, te_ref[i], j, slot, bn):
            c.wait()
        last_run = run_ref[i] == n_runs - 1

        @pl.when(jnp.logical_not(jnp.logical_and(last_run, j == pl.num_programs(0) - 1)))
        def _():
            nj = j + last_run.astype(jnp.int32)
            for c in _weight_copies(w_refs, wbuf, sems, nxt_ref[i], nj, 1 - slot, bn):
                c.start()

    return slot


def _moe_up_kernel(*refs, bn):
    tables = refs[:N_TILE_TABLES]
    xs_ref, wg_ref, wu_ref, o_ref, wbuf, sems = refs[N_TILE_TABLES:]
    slot = _run_weights(tables, (wg_ref, wu_ref), wbuf, sems, bn)

    @pl.when(pl.program_id(1) < tables[1][0])
    def _():
        h = xs_ref[...].astype(BF16)
        for c in range(bn // MXU_COLS):
            cs = slice(c * MXU_COLS, (c + 1) * MXU_COLS)
            a = _dot(h, wbuf[slot, 0, :, cs].astype(BF16))
            b = _dot(h, wbuf[slot, 1, :, cs].astype(BF16))
            o_ref[:, cs] = (_silu(a) * b).astype(o_ref.dtype)

    @pl.when(pl.program_id(1) >= tables[1][0])
    def _():
        o_ref[...] = jnp.zeros_like(o_ref)


def _moe_up(tables, xs, wg, wu, bm):
    rows, k = xs.shape
    f = wg.shape[2]
    bn = _blk(f, 1024)
    return pl.pallas_call(
        functools.partial(_moe_up_kernel, bn=bn),
        out_shape=jax.ShapeDtypeStruct((rows, f), BF16),
        grid_spec=pltpu.PrefetchScalarGridSpec(
            num_scalar_prefetch=N_TILE_TABLES,
            grid=(f // bn, rows // bm),
            in_specs=[
                pl.BlockSpec((bm, k), lambda j, i, *_: (i, 0)),
                pl.BlockSpec(memory_space=pl.ANY),
                pl.BlockSpec(memory_space=pl.ANY),
            ],
            out_specs=pl.BlockSpec((bm, bn), lambda j, i, *_: (i, j)),
            scratch_shapes=[pltpu.VMEM((2, 2, k, bn), F32), pltpu.SemaphoreType.DMA((2, 2))],
        ),
        compiler_params=_cparams("arbitrary", "arbitrary"),
        name="moe_up",
    )(*tables, xs, wg, wu)


def _moe_down_kernel(*refs, bn):
    tables = refs[:N_TILE_TABLES]
    a_ref, wd_ref, o_ref, wbuf, sems = refs[N_TILE_TABLES:]
    slot = _run_weights(tables, (wd_ref,), wbuf, sems, bn)

    @pl.when(pl.program_id(1) < tables[1][0])
    def _():
        a = a_ref[...]
        for c in range(bn // MXU_COLS):
            cs = slice(c * MXU_COLS, (c + 1) * MXU_COLS)
            o_ref[:, cs] = _dot(a, wbuf[slot, 0, :, cs].astype(BF16))

    @pl.when(pl.program_id(1) >= tables[1][0])
    def _():
        o_ref[...] = jnp.zeros_like(o_ref)


def _moe_down(tables, act, wd, bm):
    rows, f = act.shape
    n = wd.shape[2]
    bn = _blk(n, 512)
    return pl.pallas_call(
        functools.partial(_moe_down_kernel, bn=bn),
        out_shape=jax.ShapeDtypeStruct((rows, n), F32),
        grid_spec=pltpu.PrefetchScalarGridSpec(
            num_scalar_prefetch=N_TILE_TABLES,
            grid=(n // bn, rows // bm),
            in_specs=[
                pl.BlockSpec((bm, f), lambda j, i, *_: (i, 0)),
                pl.BlockSpec(memory_space=pl.ANY),
            ],
            out_specs=pl.BlockSpec((bm, bn), lambda j, i, *_: (i, j)),
            scratch_shapes=[pltpu.VMEM((2, 1, f, bn), F32), pltpu.SemaphoreType.DMA((2, 1))],
        ),
        compiler_params=_cparams("arbitrary", "arbitrary"),
        name="moe_down",
    )(*tables, act, wd)


def _combine_kernel(pos_ref, x_ref, gw_ref, fg_ref, ys_ref, o_ref, buf_ref, sems, *, rb, n_tok):
    i = pl.program_id(0)
    slot = lax.rem(i, 2)

    def gather(block, to_slot):
        def issue(r, carry):
            for k in range(TOP_K):
                _row_copy(ys_ref, buf_ref.at[to_slot, k], pos_ref[k * n_tok + block * rb + r], r,
                          sems.at[to_slot]).start()
            return carry

        lax.fori_loop(0, rb, issue, 0, unroll=8)

    @pl.when(i == 0)
    def _():
        gather(0, 0)

    @pl.when(i + 1 < pl.num_programs(0))
    def _():
        gather(i + 1, 1 - slot)

    def drain(r, carry):
        for k in range(TOP_K):
            _row_copy(ys_ref, buf_ref.at[slot, k], 0, 0, sems.at[slot]).wait()
        return carry

    lax.fori_loop(0, rb, drain, 0, unroll=8)

    gw = gw_ref[...]
    y = x_ref[...] + gw[:, 0:1] * buf_ref[slot, 0] + gw[:, 1:2] * buf_ref[slot, 1]
    o_ref[...] = _rms(y, fg_ref[...])


def _combine_norm(pos, x, gw, fg, ys):
    n_tok, d = x.shape
    rb = _blk(n_tok, 256)
    return pl.pallas_call(
        functools.partial(_combine_kernel, rb=rb, n_tok=n_tok),
        out_shape=jax.ShapeDtypeStruct((n_tok, d), F32),
        grid_spec=pltpu.PrefetchScalarGridSpec(
            num_scalar_prefetch=1,
            grid=(n_tok // rb,),
            in_specs=[
                pl.BlockSpec((rb, d), lambda i, pos: (i, 0)),
                pl.BlockSpec((rb, LANES), lambda i, pos: (i, 0)),
                pl.BlockSpec((1, d), lambda i, pos: (0, 0)),
                pl.BlockSpec(memory_space=pl.ANY),
            ],
            out_specs=pl.BlockSpec((rb, d), lambda i, pos: (i, 0)),
            scratch_shapes=[pltpu.VMEM((2, TOP_K, rb, d), F32), pltpu.SemaphoreType.DMA((2,))],
        ),
        compiler_params=_cparams("arbitrary"),
        name="moe_combine_norm",
    )(pos, x, gw, fg, ys)


def _moe_block(x, norm_g, final_g, w_router, w_gate, w_up, w_down):
    n_tok, d = x.shape
    n_experts = w_router.shape[1]
    bm = _blk(n_tok, 512)
    wr = jnp.zeros((d, LANES), F32).at[:, :n_experts].set(w_router)
    idx, gw = _router(x, norm_g, wr, n_experts)

    e_flat = idx[:, :TOP_K].T.reshape(TOP_K * n_tok)
    rank, counts = _ranks(e_flat, n_experts)

    tiles = (counts + bm - 1) // bm
    tile_end = jnp.cumsum(tiles)
    row_off = (tile_end - tiles) * bm
    pos = row_off[e_flat] + rank
    n_tiles = TOP_K * n_tok // bm + n_experts
    n_used = tile_end[-1]
    tile_ids = jnp.minimum(jnp.arange(n_tiles, dtype=jnp.int32), n_used - 1)
    te = jnp.sum(tile_ids[:, None] >= tile_end[None, :], axis=1).astype(jnp.int32)
    nu = n_used.reshape(1).astype(jnp.int32)

    gap_start = jnp.concatenate([row_off + counts, n_used[None] * bm]).astype(jnp.int32)
    gap_len = jnp.concatenate([tiles * bm - counts, (n_tiles - n_used)[None] * bm]).astype(jnp.int32)
    xs = _scatter_rows(pos, gap_start, gap_len, x, norm_g, n_tiles * bm)
    tables = _tile_tables(te, nu)
    act = _moe_up(tables, xs, w_gate, w_up, bm)
    ys = _moe_down(tables, act, w_down, bm)
    return _combine_norm(pos, x, gw, final_g, ys)


def _conformer_layer(x, bsz, seq, mix_g, ffn_g, pw1_w, pw1_b, dw_w, dw_b, ln_g, ln_b,
                     pw2_w, pw2_b, w_gate, w_up, w_down):
    d = x.shape[1]
    row = lambda v: v.reshape(1, -1)
    glu = _norm_glu(x, row(mix_g), pw1_w.astype(BF16), row(pw1_b), F32)
    hc = _dwconv(glu.reshape(bsz, seq, d), dw_w, row(dw_b), (row(ln_g), row(ln_b)), BF16)
    x = _mm_res(hc.reshape(bsz * seq, d), pw2_w.astype(BF16), row(pw2_b), x)
    act = _norm_swiglu(x, row(ffn_g), w_gate.astype(BF16), w_up.astype(BF16))
    return _mm_res(act, w_down.astype(BF16), jnp.zeros((1, d), F32), x)


def _mamba_mixer(x, bsz, seq, mix_g, w_in, conv_w, conv_b, dt_bias, a_log, d_skip, norm_g, w_out):
    d = x.shape[1]
    heads = dt_bias.shape[0]
    d_inner = norm_g.shape[0]
    conv_dim = conv_w.shape[1]
    n_state = (conv_dim - d_inner) // (2 * SSM_GROUPS)
    hd = d_inner // heads
    row = lambda v: v.reshape(1, -1)
    n_main = d_inner + conv_dim
    w_dt = jnp.zeros((d, LANES), BF16).at[:, :heads].set(w_in[:, n_main:].astype(BF16))
    zx, dt_raw = _norm_inproj(x, row(mix_g), w_in.astype(BF16), w_dt, n_main)
    xbc = _dwconv(zx.reshape(bsz, seq, n_main), conv_w, row(conv_b), None, F32, c_start=d_inner)
    dt, acs_t, acs = _ssd_decay(dt_raw[:, :heads].T, dt_bias.reshape(heads, 1),
                                a_log.reshape(heads, 1), seq=seq)
    y = _ssd(zx, xbc.reshape(bsz * seq, conv_dim), dt, acs_t, acs,
             row(jnp.repeat(d_skip, hd)), row(norm_g),
             bsz=bsz, seq=seq, d_inner=d_inner, n_state=n_state)
    return _mm_res(y, w_out.astype(BF16), jnp.zeros((1, d), F32), x)


def kernel(x, norm_mix_g, norm_ffn_g, final_norm_g, conf_pw1_w, conf_pw1_b, conf_dw_w, conf_dw_b, conf_ln_g, conf_ln_b, conf_pw2_w, conf_pw2_b, ssm_in_w, ssm_conv_w, ssm_conv_b, ssm_dt_bias, ssm_a_log, ssm_d, ssm_norm_g, ssm_out_w, ffn_w_gate, ffn_w_up, ffn_w_down, moe_router_w, moe_w_gate, moe_w_up, moe_w_down):
    bsz, seq, d = x.shape
    assert norm_mix_g.shape[0] == 2, "two layers: Conformer conv + SwiGLU, then Mamba-2 + MoE"
    xf = x.reshape(bsz * seq, d)
    xf = _conformer_layer(xf, bsz, seq, norm_mix_g[0], norm_ffn_g[0], conf_pw1_w[0], conf_pw1_b[0],
                          conf_dw_w[0], conf_dw_b[0], conf_ln_g[0], conf_ln_b[0], conf_pw2_w[0],
                          conf_pw2_b[0], ffn_w_gate[0], ffn_w_up[0], ffn_w_down[0])
    xf = _mamba_mixer(xf, bsz, seq, norm_mix_g[1], ssm_in_w[0], ssm_conv_w[0], ssm_conv_b[0],
                      ssm_dt_bias[0], ssm_a_log[0], ssm_d[0], ssm_norm_g[0], ssm_out_w[0])
    out = _moe_block(xf, norm_ffn_g[1].reshape(1, d), final_norm_g.reshape(1, d),
                     moe_router_w[0], moe_w_gate.reshape(moe_w_gate.shape[1:]),
                     moe_w_up.reshape(moe_w_up.shape[1:]), moe_w_down.reshape(moe_w_down.shape[1:]))
    return out.reshape(bsz, seq, d)
```

```python
import functools

import jax
import jax.numpy as jnp
from jax import lax
from jax.experimental import pallas as pl
from jax.experimental.pallas import tpu as pltpu

EPS = 1e-6
SSM_GROUPS = 8
SSM_CHUNK = 256
TOP_K = 2
LANES = 128
SUBLANES = 8
MXU_COLS = 256
LOG2_E = 1.4426950408889634
VMEM_LIMIT = 56 << 20
RESIDENT_WEIGHT_BYTES = 8 << 20

F32 = jnp.float32
BF16 = jnp.bfloat16


def _cparams(*sem):
    return pltpu.CompilerParams(dimension_semantics=sem, vmem_limit_bytes=VMEM_LIMIT)


def _sigmoid(v):
    return 1.0 / (1.0 + jnp.exp(-v))


def _silu(v):
    return v * _sigmoid(v)


def _rms(x, g):
    return x * lax.rsqrt(jnp.mean(x * x, axis=-1, keepdims=True) + EPS) * g


def _dot(a, b):
    return jnp.dot(a, b, preferred_element_type=F32)


def _dot_nt(a, b):
    return lax.dot_general(a, b, (((1,), (1,)), ((), ())), preferred_element_type=F32)


def _blk(n, want):
    b = min(n, want)
    assert n % b == 0, (n, want)
    return b


def _norm_glu_kernel(x_ref, g_ref, wa_ref, wb_ref, ba_ref, bb_ref, o_ref, h_ref):
    @pl.when(pl.program_id(1) == 0)
    def _():
        h_ref[...] = _rms(x_ref[...], g_ref[...]).astype(BF16)

    h = h_ref[...]
    a = _dot(h, wa_ref[...]) + ba_ref[...]
    b = _dot(h, wb_ref[...]) + bb_ref[...]
    o_ref[...] = (a * _sigmoid(b)).astype(o_ref.dtype)


def _norm_glu(x, g, w, b, out_dtype):
    m, k = x.shape
    n = w.shape[1] // 2
    bm, bn = _blk(m, 1024), _blk(n, 1024)
    nb = n // bn
    return pl.pallas_call(
        _norm_glu_kernel,
        out_shape=jax.ShapeDtypeStruct((m, n), out_dtype),
        grid=(m // bm, nb),
        in_specs=[
            pl.BlockSpec((bm, k), lambda i, j: (i, 0)),
            pl.BlockSpec((1, k), lambda i, j: (0, 0)),
            pl.BlockSpec((k, bn), lambda i, j: (0, j)),
            pl.BlockSpec((k, bn), lambda i, j: (0, j + nb)),
            pl.BlockSpec((1, bn), lambda i, j: (0, j)),
            pl.BlockSpec((1, bn), lambda i, j: (0, j + nb)),
        ],
        out_specs=pl.BlockSpec((bm, bn), lambda i, j: (i, j)),
        scratch_shapes=[pltpu.VMEM((bm, k), BF16)],
        compiler_params=_cparams("parallel", "arbitrary"),
        name="norm_glu",
    )(x, g, w, w, b, b)


def _norm_swiglu_kernel(x_ref, g_ref, wg_ref, wu_ref, o_ref, h_ref):
    @pl.when(pl.program_id(1) == 0)
    def _():
        h_ref[...] = _rms(x_ref[...], g_ref[...]).astype(BF16)

    h = h_ref[...]
    a = _dot(h, wg_ref[...])
    b = _dot(h, wu_ref[...])
    o_ref[...] = (_silu(a) * b).astype(o_ref.dtype)


def _norm_swiglu(x, g, wg, wu):
    m, k = x.shape
    n = wg.shape[1]
    bm, bn = _blk(m, 1024), _blk(n, 1024)
    return pl.pallas_call(
        _norm_swiglu_kernel,
        out_shape=jax.ShapeDtypeStruct((m, n), BF16),
        grid=(m // bm, n // bn),
        in_specs=[
            pl.BlockSpec((bm, k), lambda i, j: (i, 0)),
            pl.BlockSpec((1, k), lambda i, j: (0, 0)),
            pl.BlockSpec((k, bn), lambda i, j: (0, j)),
            pl.BlockSpec((k, bn), lambda i, j: (0, j)),
        ],
        out_specs=pl.BlockSpec((bm, bn), lambda i, j: (i, j)),
        scratch_shapes=[pltpu.VMEM((bm, k), BF16)],
        compiler_params=_cparams("parallel", "arbitrary"),
        name="norm_swiglu",
    )(x, g, wg, wu)


def _norm_inproj_kernel(x_ref, g_ref, w_ref, wdt_ref, o_ref, dt_ref, h_ref):
    @pl.when(pl.program_id(1) == 0)
    def _():
        h = _rms(x_ref[...], g_ref[...]).astype(BF16)
        h_ref[...] = h
        dt_ref[...] = _dot(h, wdt_ref[...])

    o_ref[...] = _dot(h_ref[...], w_ref[...])


def _norm_inproj(x, g, w, wdt, n):
    m, k = x.shape
    bm, bn = _blk(m, 1024), _blk(n, 1024)
    return pl.pallas_call(
        _norm_inproj_kernel,
        out_shape=(jax.ShapeDtypeStruct((m, n), F32),
                   jax.ShapeDtypeStruct((m, wdt.shape[1]), F32)),
        grid=(m // bm, n // bn),
        in_specs=[
            pl.BlockSpec((bm, k), lambda i, j: (i, 0)),
            pl.BlockSpec((1, k), lambda i, j: (0, 0)),
            pl.BlockSpec((k, bn), lambda i, j: (0, j)),
            pl.BlockSpec((k, wdt.shape[1]), lambda i, j: (0, 0)),
        ],
        out_specs=(pl.BlockSpec((bm, bn), lambda i, j: (i, j)),
                   pl.BlockSpec((bm, wdt.shape[1]), lambda i, j: (i, 0))),
        scratch_shapes=[pltpu.VMEM((bm, k), BF16)],
        compiler_params=_cparams("parallel", "arbitrary"),
        name="norm_inproj",
    )(x, g, w, wdt)


def _mm_res_kernel(a_ref, w_ref, b_ref, r_ref, o_ref):
    o_ref[...] = r_ref[...] + _dot(a_ref[...], w_ref[...]) + b_ref[...]


def _mm_res(a, w, bias, res):
    m, k = a.shape
    n = w.shape[1]
    if k * n * 2 <= RESIDENT_WEIGHT_BYTES:
        bm, bn = _blk(m, 512), n
    else:
        bm, bn = _blk(m, 1024), _blk(n, 256 if k > 4096 else 512)
    return pl.pallas_call(
        _mm_res_kernel,
        out_shape=jax.ShapeDtypeStruct((m, n), F32),
        grid=(m // bm, n // bn),
        in_specs=[
            pl.BlockSpec((bm, k), lambda i, j: (i, 0)),
            pl.BlockSpec((k, bn), lambda i, j: (0, j)),
            pl.BlockSpec((1, bn), lambda i, j: (0, j)),
            pl.BlockSpec((bm, bn), lambda i, j: (i, j)),
        ],
        out_specs=pl.BlockSpec((bm, bn), lambda i, j: (i, j)),
        compiler_params=_cparams("parallel", "arbitrary"),
        name="mm_res",
    )(a, w, bias, res)


def _conv_rows(taps):
    return 16 if taps > SUBLANES else 32


def _dwconv_kernel(*refs, taps, halo, bt, cb, layer_norm):
    rows = _conv_rows(taps)
    if layer_norm:
        xc_ref, xh_ref, w_ref, b_ref, lg_ref, lb_ref, o_ref, win_ref, y_ref = refs
    else:
        xc_ref, xh_ref, w_ref, b_ref, o_ref, win_ref = refs
        y_ref = None
    first = pl.program_id(1) == 0
    win_ref[0, 0:halo, :] = jnp.where(first, 0.0, xh_ref[...])
    win_ref[0, halo:halo + bt, :] = xc_ref[...]
    off = halo - (taps - 1)
    n_sh = halo + bt - SUBLANES
    for s in sorted({(off + j) % SUBLANES for j in range(taps)} - {0}):
        win_ref[s, 0:n_sh, :] = win_ref[0, s:s + n_sh, :]

    def body(r, carry):
        r0 = pl.multiple_of(r * rows, rows)
        for c in range(cb // LANES):
            cs = slice(c * LANES, (c + 1) * LANES)
            acc = jnp.broadcast_to(b_ref[:, cs], (rows, LANES))
            for j in range(taps):
                al, s = divmod(off + j, SUBLANES)
                xw = win_ref[s, pl.ds(r0 + al * SUBLANES, rows), cs]
                acc = acc + xw * w_ref[j:j + 1, cs]
            if layer_norm:
                y_ref[pl.ds(r0, rows), cs] = acc
            else:
                o_ref[pl.ds(r0, rows), cs] = _silu(acc).astype(o_ref.dtype)
        return carry

    lax.fori_loop(0, bt // rows, body, 0)

    if layer_norm:
        y = y_ref[...]
        mu = jnp.mean(y, axis=-1, keepdims=True)
        d = y - mu
        var = jnp.mean(d * d, axis=-1, keepdims=True)
        hn = d * lax.rsqrt(var + EPS) * lg_ref[...] + lb_ref[...]
        o_ref[...] = _silu(hn).astype(o_ref.dtype)


def _dwconv(x3, w, b, ln, out_dtype, c_start=0):
    bsz, seq, _ = x3.shape
    taps, ch = w.shape
    halo = SUBLANES * (-(-(taps - 1) // SUBLANES))
    bt = _blk(seq, 256)
    assert bt % halo == 0 and bt % _conv_rows(taps) == 0
    layer_norm = ln is not None
    cb = ch if layer_norm else _blk(ch, 1024)
    assert c_start % cb == 0
    co = c_start // cb
    hb = bt // halo
    in_specs = [
        pl.BlockSpec((None, bt, cb), lambda bi, i, c: (bi, i, c + co)),
        pl.BlockSpec((None, halo, cb),
                     lambda bi, i, c: (bi, jnp.maximum(i * hb - 1, 0), c + co)),
        pl.BlockSpec((taps, cb), lambda bi, i, c: (0, c)),
        pl.BlockSpec((1, cb), lambda bi, i, c: (0, c)),
    ]
    args = [x3, x3, w, b]
    scratch = [pltpu.VMEM((SUBLANES, halo + bt, cb), F32)]
    if layer_norm:
        in_specs += [pl.BlockSpec((1, cb), lambda bi, i, c: (0, c))] * 2
        args += [ln[0], ln[1]]
        scratch.append(pltpu.VMEM((bt, cb), F32))
    return pl.pallas_call(
        functools.partial(_dwconv_kernel, taps=taps, halo=halo, bt=bt, cb=cb,
                          layer_norm=layer_norm),
        out_shape=jax.ShapeDtypeStruct((bsz, seq, ch), out_dtype),
        grid=(bsz, seq // bt, ch // cb),
        in_specs=in_specs,
        out_specs=pl.BlockSpec((None, bt, cb), lambda bi, i, c: (bi, i, c)),
        scratch_shapes=scratch,
        compiler_params=_cparams("parallel", "parallel", "parallel"),
        name="dwconv_ln" if layer_norm else "dwconv",
    )(*args)


SSD_ROWS = 64


def _split3(v):
    hi = v.astype(BF16)
    r = v - hi.astype(F32)
    mid = r.astype(BF16)
    lo = (r - mid.astype(F32)).astype(BF16)
    return hi, mid, lo


def _expand_heads(v, hpg, hd):
    rows = v.shape[0]
    lane_head = lax.broadcasted_iota(jnp.int32, (rows, hpg * hd), 1) // hd
    out = jnp.zeros((rows, hpg * hd), F32)
    for h in range(hpg):
        out = jnp.where(lane_head == h, v[:, h:h + 1], out)
    return out


def _ssd_decay_kernel(dtt_ref, dtb_ref, alog_ref, dt_ref, acst_ref, acs_ref, *, hpg, q):
    v = dtt_ref[...] + dtb_ref[...]
    dt = jnp.maximum(v, 0.0) + jnp.log1p(jnp.exp(-jnp.abs(v)))
    a = dt * (-LOG2_E * jnp.exp(alog_ref[...]))
    dt_ref[...] = dt

    rr = lax.broadcasted_iota(jnp.int32, (q, q), 0)
    cc = lax.broadcasted_iota(jnp.int32, (q, q), 1)
    tri_ls = jnp.where(rr >= cc, 1.0, 0.0).astype(BF16)
    tri_sl = jnp.where(rr <= cc, 1.0, 0.0).astype(BF16)
    acs_t = jnp.zeros(a.shape, F32)
    for piece in _split3(a):
        acs_t = acs_t + _dot(piece, tri_sl)
    acst_ref[...] = acs_t
    heads = a.shape[0]
    a_pad = jnp.concatenate([a, jnp.zeros((LANES - heads, q), F32)], axis=0) if heads < LANES else a
    acs = jnp.zeros((q, LANES), F32)
    for piece in _split3(a_pad):
        acs = acs + _dot_nt(tri_ls, piece)
    for g in range(heads // hpg):
        rolled = acs if g == 0 else pltpu.roll(acs, LANES - g * hpg, axis=1)
        acs_ref[:, g * LANES:(g + 1) * LANES] = rolled


def _ssd_decay(dtt, dt_bias, a_log, *, seq):
    heads, n_tok = dtt.shape
    hpg = heads // SSM_GROUPS
    q = _blk(seq, SSM_CHUNK)
    return pl.pallas_call(
        functools.partial(_ssd_decay_kernel, hpg=hpg, q=q),
        out_shape=(jax.ShapeDtypeStruct((heads, n_tok), F32),
                   jax.ShapeDtypeStruct((heads, n_tok), F32),
                   jax.ShapeDtypeStruct((n_tok, SSM_GROUPS * LANES), F32)),
        grid=(n_tok // q,),
        in_specs=[
            pl.BlockSpec((heads, q), lambda r: (0, r)),
            pl.BlockSpec((heads, 1), lambda r: (0, 0)),
            pl.BlockSpec((heads, 1), lambda r: (0, 0)),
        ],
        out_specs=(pl.BlockSpec((heads, q), lambda r: (0, r)),
                   pl.BlockSpec((heads, q), lambda r: (0, r)),
                   pl.BlockSpec((q, SSM_GROUPS * LANES), lambda r: (r, 0))),
        compiler_params=_cparams("parallel"),
        name="ssd_decay",
    )(dtt, dt_bias, a_log)


def _ssd_kernel(x_ref, b_ref, c_ref, z_ref, dt_ref, acst_ref, acs_ref, dexp_ref, ng_ref,
                o_ref, s_ref, y_ref, *, hpg, hd, q):
    @pl.when(pl.program_id(2) == 0)
    def _():
        s_ref[...] = jnp.zeros_like(s_ref)

    xg = x_ref[...]
    bg = b_ref[...]
    cgb = c_ref[...].astype(BF16)
    dt = dt_ref[...]
    acs_t = acst_ref[...]
    acs = acs_ref[...]
    wst_t = jnp.exp2(acs_t[:, q - 1:q] - acs_t) * dt

    cb = _dot_nt(cgb, bg.astype(BF16))
    bt = bg.T
    pair = LANES // hd
    lane_y = lax.broadcasted_iota(jnp.int32, (SSD_ROWS, LANES), 1)
    lane_s = lax.broadcasted_iota(jnp.int32, (bt.shape[0], LANES), 1)
    s_old = s_ref[...]
    for t in range(hpg // pair):
        ts = slice(t * LANES, (t + 1) * LANES)
        x_t = xg[:, ts].astype(BF16)
        s_t = None
        for p in range(pair):
            h = t * pair + p
            s_h = _dot((bt * wst_t[h:h + 1, :]).astype(BF16), x_t)
            s_t = s_h if p == 0 else jnp.where(lane_s >= p * hd, s_h, s_t)
        for rc in range(q // SSD_ROWS):
            rs = slice(rc * SSD_ROWS, (rc + 1) * SSD_ROWS)
            kc = min(q, LANES * (-(-(rc + 1) * SSD_ROWS // LANES)))
            causal_c = (lax.broadcasted_iota(jnp.int32, (SSD_ROWS, kc), 0) + rc * SSD_ROWS
                        >= lax.broadcasted_iota(jnp.int32, (SSD_ROWS, kc), 1))
            y_c = None
            for p in range(pair):
                h = t * pair + p
                seg = acs[rs, h:h + 1] - acs_t[h:h + 1, :kc]
                decay = jnp.exp2(jnp.where(causal_c, seg, -jnp.inf))
                m = (cb[rs, :kc] * decay * dt[h:h + 1, :kc]).astype(BF16)
                y_h = _dot(m, x_t[:kc])
                y_c = y_h if p == 0 else jnp.where(lane_y >= p * hd, y_h, y_c)
            y_ref[rs, ts] = y_c
        dec_t = _expand_heads(jnp.exp2(acs[q - 1:q, t * pair:(t + 1) * pair]), pair, hd)
        s_ref[:, ts] = s_old[:, ts] * dec_t + s_t

    y = y_ref[...]
    y = y + _dot(cgb, s_old.astype(BF16)) * _expand_heads(jnp.exp2(acs[:, 0:hpg]), hpg, hd)
    y = y + xg * dexp_ref[...]
    y = y * _silu(z_ref[...])
    y = y * lax.rsqrt(jnp.mean(y * y, axis=-1, keepdims=True) + EPS) * ng_ref[...]
    o_ref[...] = y.astype(o_ref.dtype)


def _ssd(zx, xbc, dt, acs_t, acs, d_exp, norm_g, *, bsz, seq, d_inner, n_state):
    groups = SSM_GROUPS
    heads = dt.shape[0]
    hpg = heads // groups
    hd = d_inner // heads
    gw = hpg * hd
    q = _blk(seq, SSM_CHUNK)
    nc = seq // q
    assert gw % LANES == 0 and LANES % hd == 0 and n_state % LANES == 0
    assert d_inner // groups == gw
    b_off = d_inner // n_state
    c_off = b_off + groups
    row = lambda b, g, c: b * nc + c
    return pl.pallas_call(
        functools.partial(_ssd_kernel, hpg=hpg, hd=hd, q=q),
        out_shape=jax.ShapeDtypeStruct((bsz * seq, d_inner), BF16),
        grid=(bsz, groups, nc),
        in_specs=[
            pl.BlockSpec((q, gw), lambda b, g, c: (row(b, g, c), g)),
            pl.BlockSpec((q, n_state), lambda b, g, c: (row(b, g, c), b_off + g)),
            pl.BlockSpec((q, n_state), lambda b, g, c: (row(b, g, c), c_off + g)),
            pl.BlockSpec((q, gw), lambda b, g, c: (row(b, g, c), g)),
            pl.BlockSpec((hpg, q), lambda b, g, c: (g, row(b, g, c))),
            pl.BlockSpec((hpg, q), lambda b, g, c: (g, row(b, g, c))),
            pl.BlockSpec((q, LANES), lambda b, g, c: (row(b, g, c), g)),
            pl.BlockSpec((1, gw), lambda b, g, c: (0, g)),
            pl.BlockSpec((1, gw), lambda b, g, c: (0, g)),
        ],
        out_specs=pl.BlockSpec((q, gw), lambda b, g, c: (row(b, g, c), g)),
        scratch_shapes=[pltpu.VMEM((n_state, gw), F32), pltpu.VMEM((q, gw), F32)],
        compiler_params=_cparams("parallel", "parallel", "arbitrary"),
        name="ssd",
    )(xbc, xbc, xbc, zx, dt, acs_t, acs, d_exp, norm_g)


def _router_kernel(x_ref, g_ref, wr_ref, idx_ref, gw_ref, *, n_experts):
    h = _rms(x_ref[...], g_ref[...])
    w = wr_ref[...]
    hh = h.astype(BF16)
    hl = (h - hh.astype(F32)).astype(BF16)
    wh = w.astype(BF16)
    wl = (w - wh.astype(F32)).astype(BF16)
    logits = _dot(hh, wh) + _dot(hl, wh) + _dot(hh, wl)
    lane = lax.broadcasted_iota(jnp.int32, logits.shape, 1).astype(F32)
    neg = -jnp.inf
    lg = jnp.where(lane < n_experts, logits, neg)
    m1 = jnp.max(lg, axis=-1, keepdims=True)
    i1 = jnp.min(jnp.where(lg == m1, lane, float(LANES)), axis=-1, keepdims=True)
    lg2 = jnp.where(lane == i1, neg, lg)
    m2 = jnp.max(lg2, axis=-1, keepdims=True)
    i2 = jnp.min(jnp.where(lg2 == m2, lane, float(LANES)), axis=-1, keepdims=True)
    e = jnp.exp(m2 - m1)
    w1 = 1.0 / (1.0 + e)
    w2 = e / (1.0 + e)
    idx_ref[...] = jnp.where(lane == 0.0, i1, i2).astype(jnp.int32)
    gw_ref[...] = jnp.where(lane == 0.0, w1, w2)


def _router(x, g, wr, n_experts):
    m, k = x.shape
    bm = _blk(m, 512)
    return pl.pallas_call(
        functools.partial(_router_kernel, n_experts=n_experts),
        out_shape=(jax.ShapeDtypeStruct((m, LANES), jnp.int32),
                   jax.ShapeDtypeStruct((m, LANES), F32)),
        grid=(m // bm,),
        in_specs=[
            pl.BlockSpec((bm, k), lambda i: (i, 0)),
            pl.BlockSpec((1, k), lambda i: (0, 0)),
            pl.BlockSpec((k, LANES), lambda i: (0, 0)),
        ],
        out_specs=(pl.BlockSpec((bm, LANES), lambda i: (i, 0)),
                   pl.BlockSpec((bm, LANES), lambda i: (i, 0))),
        compiler_params=_cparams("parallel"),
        name="router",
    )(x, g, wr)


def _rank_kernel(e_ref, rank_ref, cnt_ref, carry_ref, *, ep, bl):
    @pl.when(pl.program_id(0) == 0)
    def _():
        carry_ref[...] = jnp.zeros_like(carry_ref)

    e_row = e_ref[0]
    sub = lax.broadcasted_iota(jnp.int32, (ep, bl), 0)
    hit = sub == e_row
    oh = jnp.where(hit, 1.0, 0.0).astype(BF16)
    rr = lax.broadcasted_iota(jnp.int32, (bl, bl), 0)
    cc = lax.broadcasted_iota(jnp.int32, (bl, bl), 1)
    tri = jnp.where(rr <= cc, 1.0, 0.0).astype(BF16)
    pre = _dot(oh, tri)
    carry = carry_ref[...]
    rank = jnp.sum(jnp.where(hit, pre - 1.0 + carry[:, 0:1], 0.0), axis=0, keepdims=True)
    rank_ref[0] = rank.astype(jnp.int32)
    carry = carry + pre[:, bl - 1:bl]
    carry_ref[...] = carry
    cnt_ref[...] = carry.astype(jnp.int32)


def _ranks(e_flat, n_experts):
    n = e_flat.shape[0]
    bl = _blk(n, 512)
    ep = 2 * SUBLANES * (-(-n_experts // (2 * SUBLANES)))
    rank, cnt = pl.pallas_call(
        functools.partial(_rank_kernel, ep=ep, bl=bl),
        out_shape=(jax.ShapeDtypeStruct((n // bl, 1, bl), jnp.int32),
                   jax.ShapeDtypeStruct((ep, LANES), jnp.int32)),
        grid=(n // bl,),
        in_specs=[pl.BlockSpec((1, 1, bl), lambda i: (i, 0, 0))],
        out_specs=(pl.BlockSpec((1, 1, bl), lambda i: (i, 0, 0)),
                   pl.BlockSpec((ep, LANES), lambda i: (0, 0))),
        scratch_shapes=[pltpu.VMEM((ep, LANES), F32)],
        compiler_params=_cparams("arbitrary"),
        name="expert_ranks",
    )(e_flat.reshape(n // bl, 1, bl))
    return rank.reshape(n), cnt[:n_experts, 0]


def _row_copy(src, dst, s_row, d_row, sem):
    return pltpu.make_async_copy(src.at[pl.ds(s_row, 1)], dst.at[pl.ds(d_row, 1)], sem)


def _scatter_kernel(pos_ref, gap0_ref, gapn_ref, x_ref, g_ref, xs_ref, h_ref, z_ref, sem, zsem,
                    *, rb, n_tok, n_gaps):
    base = pl.program_id(0) * rb

    @pl.when(pl.program_id(0) == 0)
    def _():
        z_ref[...] = jnp.zeros_like(z_ref)
        for e in range(n_gaps):
            def fill(r, carry, e=e):
                _row_copy(z_ref, xs_ref, 0, gap0_ref[e] + r, zsem).start()
                return carry

            lax.fori_loop(0, gapn_ref[e], fill, 0)

    h_ref[...] = _rms(x_ref[...], g_ref[...])

    def issue(r, carry):
        for k in range(TOP_K):
            _row_copy(h_ref, xs_ref, r, pos_ref[k * n_tok + base + r], sem).start()
        return carry

    lax.fori_loop(0, rb, issue, 0, unroll=8)

    def drain(r, carry):
        for k in range(TOP_K):
            _row_copy(h_ref, xs_ref, 0, 0, sem).wait()
        return carry

    lax.fori_loop(0, rb, drain, 0, unroll=8)

    @pl.when(pl.program_id(0) == pl.num_programs(0) - 1)
    def _():
        for e in range(n_gaps):
            def fill_done(r, carry):
                _row_copy(z_ref, xs_ref, 0, 0, zsem).wait()
                return carry

            lax.fori_loop(0, gapn_ref[e], fill_done, 0)


def _scatter_rows(pos, gap_start, gap_len, x, g, n_rows):
    n_tok, k = x.shape
    rb = _blk(n_tok, 256)
    return pl.pallas_call(
        functools.partial(_scatter_kernel, rb=rb, n_tok=n_tok, n_gaps=gap_start.shape[0]),
        out_shape=jax.ShapeDtypeStruct((n_rows, k), F32),
        grid_spec=pltpu.PrefetchScalarGridSpec(
            num_scalar_prefetch=3,
            grid=(n_tok // rb,),
            in_specs=[pl.BlockSpec((rb, k), lambda i, *_: (i, 0)),
                      pl.BlockSpec((1, k), lambda i, *_: (0, 0))],
            out_specs=pl.BlockSpec(memory_space=pl.ANY),
            scratch_shapes=[pltpu.VMEM((rb, k), F32), pltpu.VMEM((SUBLANES, k), F32),
                            pltpu.SemaphoreType.DMA, pltpu.SemaphoreType.DMA],
        ),
        compiler_params=_cparams("arbitrary"),
        name="moe_scatter",
    )(pos, gap_start, gap_len, x, g)


N_TILE_TABLES = 6


def _tile_tables(te, nu):
    first = jnp.concatenate([jnp.ones((1,), jnp.int32), (te[1:] != te[:-1]).astype(jnp.int32)])
    run = jnp.cumsum(first) - 1
    later = jnp.where(te[None, :] > te[:, None], te[None, :], jnp.iinfo(jnp.int32).max)
    nxt = jnp.min(later, axis=1)
    nxt = jnp.where(nxt == jnp.iinfo(jnp.int32).max, te[0], nxt)
    return (te, nu, first, run.astype(jnp.int32), nxt.astype(jnp.int32),
            (run[-1:] + 1).astype(jnp.int32))


def _weight_copies(w_refs, wbuf, sems, e, j, slot, bn):
    cols = pl.ds(pl.multiple_of(j * bn, bn), bn)
    return [pltpu.make_async_copy(w.at[e, :, cols], wbuf.at[slot, n], sems.at[slot, n])
            for n, w in enumerate(w_refs)]


def _run_weights(tables, w_refs, wbuf, sems, bn):
    te_ref, _, first_ref, run_ref, nxt_ref, nruns_ref = tables
    j, i = pl.program_id(0), pl.program_id(1)
    n_runs = nruns_ref[0]
    g = j * n_runs + run_ref[i]
    slot = lax.rem(g, 2)

    @pl.when(first_ref[i] == 1)
    def _():
        @pl.when(g == 0)
        def _():
            for c in _weight_copies(w_refs, wbuf, sems, te_ref[0], 0, 0, bn):
                c.start()

        for c in _weight_copies(w_refs, wbuf, sems, te_ref[i], j, slot, bn):
            c.wait()
        last_run = run_ref[i] == n_runs - 1

        @pl.when(jnp.logical_not(jnp.logical_and(last_run, j == pl.num_programs(0) - 1)))
        def _():
            nj = j + last_run.astype(jnp.int32)
            for c in _weight_copies(w_refs, wbuf, sems, nxt_ref[i], nj, 1 - slot, bn):
                c.start()

    return slot


def _moe_up_kernel(*refs, bn):
    tables = refs[:N_TILE_TABLES]
    xs_ref, wg_ref, wu_ref, o_ref, wbuf, sems = refs[N_TILE_TABLES:]
    slot = _run_weights(tables, (wg_ref, wu_ref), wbuf, sems, bn)

    @pl.when(pl.program_id(1) < tables[1][0])
    def _():
        h = xs_ref[...].astype(BF16)
        for c in range(bn // MXU_COLS):
            cs = slice(c * MXU_COLS, (c + 1) * MXU_COLS)
            a = _dot(h, wbuf[slot, 0, :, cs].astype(BF16))
            b = _dot(h, wbuf[slot, 1, :, cs].astype(BF16))
            o_ref[:, cs] = (_silu(a) * b).astype(o_ref.dtype)

    @pl.when(pl.program_id(1) >= tables[1][0])
    def _():
        o_ref[...] = jnp.zeros_like(o_ref)


def _moe_up(tables, xs, wg, wu, bm):
    rows, k = xs.shape
    f = wg.shape[2]
    bn = _blk(f, 1024)
    return pl.pallas_call(
        functools.partial(_moe_up_kernel, bn=bn),
        out_shape=jax.ShapeDtypeStruct((rows, f), BF16),
        grid_spec=pltpu.PrefetchScalarGridSpec(
            num_scalar_prefetch=N_TILE_TABLES,
            grid=(f // bn, rows // bm),
            in_specs=[
                pl.BlockSpec((bm, k), lambda j, i, *_: (i, 0)),
                pl.BlockSpec(memory_space=pl.ANY),
                pl.BlockSpec(memory_space=pl.ANY),
            ],
            out_specs=pl.BlockSpec((bm, bn), lambda j, i, *_: (i, j)),
            scratch_shapes=[pltpu.VMEM((2, 2, k, bn), F32), pltpu.SemaphoreType.DMA((2, 2))],
        ),
        compiler_params=_cparams("arbitrary", "arbitrary"),
        name="moe_up",
    )(*tables, xs, wg, wu)


def _moe_down_kernel(*refs, bn):
    tables = refs[:N_TILE_TABLES]
    a_ref, wd_ref, o_ref, wbuf, sems = refs[N_TILE_TABLES:]
    slot = _run_weights(tables, (wd_ref,), wbuf, sems, bn)

    @pl.when(pl.program_id(1) < tables[1][0])
    def _():
        a = a_ref[...]
        for c in range(bn // MXU_COLS):
            cs = slice(c * MXU_COLS, (c + 1) * MXU_COLS)
            o_ref[:, cs] = _dot(a, wbuf[slot, 0, :, cs].astype(BF16))

    @pl.when(pl.program_id(1) >= tables[1][0])
    def _():
        o_ref[...] = jnp.zeros_like(o_ref)


def _moe_down(tables, act, wd, bm):
    rows, f = act.shape
    n = wd.shape[2]
    bn = _blk(n, 512)
    return pl.pallas_call(
        functools.partial(_moe_down_kernel, bn=bn),
        out_shape=jax.ShapeDtypeStruct((rows, n), F32),
        grid_spec=pltpu.PrefetchScalarGridSpec(
            num_scalar_prefetch=N_TILE_TABLES,
            grid=(n // bn, rows // bm),
            in_specs=[
                pl.BlockSpec((bm, f), lambda j, i, *_: (i, 0)),
                pl.BlockSpec(memory_space=pl.ANY),
            ],
            out_specs=pl.BlockSpec((bm, bn), lambda j, i, *_: (i, j)),
            scratch_shapes=[pltpu.VMEM((2, 1, f, bn), F32), pltpu.SemaphoreType.DMA((2, 1))],
        ),
        compiler_params=_cparams("arbitrary", "arbitrary"),
        name="moe_down",
    )(*tables, act, wd)


def _combine_kernel(pos_ref, x_ref, gw_ref, fg_ref, ys_ref, o_ref, buf_ref, sems, *, rb, n_tok):
    i = pl.program_id(0)
    slot = lax.rem(i, 2)

    def gather(block, to_slot):
        def issue(r, carry):
            for k in range(TOP_K):
                _row_copy(ys_ref, buf_ref.at[to_slot, k], pos_ref[k * n_tok + block * rb + r], r,
                          sems.at[to_slot]).start()
            return carry

        lax.fori_loop(0, rb, issue, 0, unroll=8)

    @pl.when(i == 0)
    def _():
        gather(0, 0)

    @pl.when(i + 1 < pl.num_programs(0))
    def _():
        gather(i + 1, 1 - slot)

    def drain(r, carry):
        for k in range(TOP_K):
            _row_copy(ys_ref, buf_ref.at[slot, k], 0, 0, sems.at[slot]).wait()
        return carry

    lax.fori_loop(0, rb, drain, 0, unroll=8)

    gw = gw_ref[...]
    y = x_ref[...] + gw[:, 0:1] * buf_ref[slot, 0] + gw[:, 1:2] * buf_ref[slot, 1]
    o_ref[...] = _rms(y, fg_ref[...])


def _combine_norm(pos, x, gw, fg, ys):
    n_tok, d = x.shape
    rb = _blk(n_tok, 256)
    return pl.pallas_call(
        functools.partial(_combine_kernel, rb=rb, n_tok=n_tok),
        out_shape=jax.ShapeDtypeStruct((n_tok, d), F32),
        grid_spec=pltpu.PrefetchScalarGridSpec(
            num_scalar_prefetch=1,
            grid=(n_tok // rb,),
            in_specs=[
                pl.BlockSpec((rb, d), lambda i, pos: (i, 0)),
                pl.BlockSpec((rb, LANES), lambda i, pos: (i, 0)),
                pl.BlockSpec((1, d), lambda i, pos: (0, 0)),
                pl.BlockSpec(memory_space=pl.ANY),
            ],
            out_specs=pl.BlockSpec((rb, d), lambda i, pos: (i, 0)),
            scratch_shapes=[pltpu.VMEM((2, TOP_K, rb, d), F32), pltpu.SemaphoreType.DMA((2,))],
        ),
        compiler_params=_cparams("arbitrary"),
        name="moe_combine_norm",
    )(pos, x, gw, fg, ys)


def _moe_block(x, norm_g, final_g, w_router, w_gate, w_up, w_down):
    n_tok, d = x.shape
    n_experts = w_router.shape[1]
    bm = _blk(n_tok, 512)
    wr = jnp.zeros((d, LANES), F32).at[:, :n_experts].set(w_router)
    idx, gw = _router(x, norm_g, wr, n_experts)

    e_flat = idx[:, :TOP_K].T.reshape(TOP_K * n_tok)
    rank, counts = _ranks(e_flat, n_experts)

    tiles = (counts + bm - 1) // bm
    tile_end = jnp.cumsum(tiles)
    row_off = (tile_end - tiles) * bm
    pos = row_off[e_flat] + rank
    n_tiles = TOP_K * n_tok // bm + n_experts
    n_used = tile_end[-1]
    tile_ids = jnp.minimum(jnp.arange(n_tiles, dtype=jnp.int32), n_used - 1)
    te = jnp.sum(tile_ids[:, None] >= tile_end[None, :], axis=1).astype(jnp.int32)
    nu = n_used.reshape(1).astype(jnp.int32)

    gap_start = jnp.concatenate([row_off + counts, n_used[None] * bm]).astype(jnp.int32)
    gap_len = jnp.concatenate([tiles * bm - counts, (n_tiles - n_used)[None] * bm]).astype(jnp.int32)
    xs = _scatter_rows(pos, gap_start, gap_len, x, norm_g, n_tiles * bm)
    tables = _tile_tables(te, nu)
    act = _moe_up(tables, xs, w_gate, w_up, bm)
    ys = _moe_down(tables, act, w_down, bm)
    return _combine_norm(pos, x, gw, final_g, ys)


def _conformer_layer(x, bsz, seq, mix_g, ffn_g, pw1_w, pw1_b, dw_w, dw_b, ln_g, ln_b,
                     pw2_w, pw2_b, w_gate, w_up, w_down):
    d = x.shape[1]
    row = lambda v: v.reshape(1, -1)
    glu = _norm_glu(x, row(mix_g), pw1_w.astype(BF16), row(pw1_b), F32)
    hc = _dwconv(glu.reshape(bsz, seq, d), dw_w, row(dw_b), (row(ln_g), row(ln_b)), BF16)
    x = _mm_res(hc.reshape(bsz * seq, d), pw2_w.astype(BF16), row(pw2_b), x)
    act = _norm_swiglu(x, row(ffn_g), w_gate.astype(BF16), w_up.astype(BF16))
    return _mm_res(act, w_down.astype(BF16), jnp.zeros((1, d), F32), x)


def _mamba_mixer(x, bsz, seq, mix_g, w_in, conv_w, conv_b, dt_bias, a_log, d_skip, norm_g, w_out):
    d = x.shape[1]
    heads = dt_bias.shape[0]
    d_inner = norm_g.shape[0]
    conv_dim = conv_w.shape[1]
    n_state = (conv_dim - d_inner) // (2 * SSM_GROUPS)
    hd = d_inner // heads
    row = lambda v: v.reshape(1, -1)
    n_main = d_inner + conv_dim
    w_dt = jnp.zeros((d, LANES), BF16).at[:, :heads].set(w_in[:, n_main:].astype(BF16))
    zx, dt_raw = _norm_inproj(x, row(mix_g), w_in.astype(BF16), w_dt, n_main)
    xbc = _dwconv(zx.reshape(bsz, seq, n_main), conv_w, row(conv_b), None, F32, c_start=d_inner)
    dt, acs_t, acs = _ssd_decay(dt_raw[:, :heads].T, dt_bias.reshape(heads, 1),
                                a_log.reshape(heads, 1), seq=seq)
    y = _ssd(zx, xbc.reshape(bsz * seq, conv_dim), dt, acs_t, acs,
             row(jnp.repeat(d_skip, hd)), row(norm_g),
             bsz=bsz, seq=seq, d_inner=d_inner, n_state=n_state)
    return _mm_res(y, w_out.astype(BF16), jnp.zeros((1, d), F32), x)


def kernel(x, norm_mix_g, norm_ffn_g, final_norm_g, conf_pw1_w, conf_pw1_b, conf_dw_w, conf_dw_b, conf_ln_g, conf_ln_b, conf_pw2_w, conf_pw2_b, ssm_in_w, ssm_conv_w, ssm_conv_b, ssm_dt_bias, ssm_a_log, ssm_d, ssm_norm_g, ssm_out_w, ffn_w_gate, ffn_w_up, ffn_w_down, moe_router_w, moe_w_gate, moe_w_up, moe_w_down):
    bsz, seq, d = x.shape
    assert norm_mix_g.shape[0] == 2, "two layers: Conformer conv + SwiGLU, then Mamba-2 + MoE"
    xf = x.reshape(bsz * seq, d)
    xf = _conformer_layer(xf, bsz, seq, norm_mix_g[0], norm_ffn_g[0], conf_pw1_w[0], conf_pw1_b[0],
                          conf_dw_w[0], conf_dw_b[0], conf_ln_g[0], conf_ln_b[0], conf_pw2_w[0],
                          conf_pw2_b[0], ffn_w_gate[0], ffn_w_up[0], ffn_w_down[0])
    xf = _mamba_mixer(xf, bsz, seq, norm_mix_g[1], ssm_in_w[0], ssm_conv_w[0], ssm_conv_b[0],
                      ssm_dt_bias[0], ssm_a_log[0], ssm_d[0], ssm_norm_g[0], ssm_out_w[0])
    out = _moe_block(xf, norm_ffn_g[1].reshape(1, d), final_norm_g.reshape(1, d),
                     moe_router_w[0], moe_w_gate.reshape(moe_w_gate.shape[1:]),
                     moe_w_up.reshape(moe_w_up.shape[1:]), moe_w_down.reshape(moe_w_down.shape[1:]))
    return out.reshape(bsz, seq, d)
```

```python
import functools

import jax
import jax.numpy as jnp
from jax import lax
from jax.experimental import pallas as pl
from jax.experimental.pallas import tpu as pltpu

EPS = 1e-6
SSM_GROUPS = 8
SSM_CHUNK = 256
TOP_K = 2
LANES = 128
SUBLANES = 8
MXU_COLS = 256
LOG2_E = 1.4426950408889634
VMEM_LIMIT = 56 << 20
RESIDENT_WEIGHT_BYTES = 8 << 20

F32 = jnp.float32
BF16 = jnp.bfloat16


def _cparams(*sem):
    return pltpu.CompilerParams(dimension_semantics=sem, vmem_limit_bytes=VMEM_LIMIT)


def _sigmoid(v):
    return 1.0 / (1.0 + jnp.exp(-v))


def _silu(v):
    return v * _sigmoid(v)


def _rms(x, g):
    return x * lax.rsqrt(jnp.mean(x * x, axis=-1, keepdims=True) + EPS) * g


def _dot(a, b):
    return jnp.dot(a, b, preferred_element_type=F32)


def _dot_nt(a, b):
    return lax.dot_general(a, b, (((1,), (1,)), ((), ())), preferred_element_type=F32)


def _blk(n, want):
    b = min(n, want)
    assert n % b == 0, (n, want)
    return b


def _norm_glu_kernel(x_ref, g_ref, wa_ref, wb_ref, ba_ref, bb_ref, o_ref, h_ref):
    @pl.when(pl.program_id(1) == 0)
    def _():
        h_ref[...] = _rms(x_ref[...], g_ref[...]).astype(BF16)

    h = h_ref[...]
    a = _dot(h, wa_ref[...]) + ba_ref[...]
    b = _dot(h, wb_ref[...]) + bb_ref[...]
    o_ref[...] = (a * _sigmoid(b)).astype(o_ref.dtype)


def _norm_glu(x, g, w, b, out_dtype):
    m, k = x.shape
    n = w.shape[1] // 2
    bm, bn = _blk(m, 1024), _blk(n, 1024)
    nb = n // bn
    return pl.pallas_call(
        _norm_glu_kernel,
        out_shape=jax.ShapeDtypeStruct((m, n), out_dtype),
        grid=(m // bm, nb),
        in_specs=[
            pl.BlockSpec((bm, k), lambda i, j: (i, 0)),
            pl.BlockSpec((1, k), lambda i, j: (0, 0)),
            pl.BlockSpec((k, bn), lambda i, j: (0, j)),
            pl.BlockSpec((k, bn), lambda i, j: (0, j + nb)),
            pl.BlockSpec((1, bn), lambda i, j: (0, j)),
            pl.BlockSpec((1, bn), lambda i, j: (0, j + nb)),
        ],
        out_specs=pl.BlockSpec((bm, bn), lambda i, j: (i, j)),
        scratch_shapes=[pltpu.VMEM((bm, k), BF16)],
        compiler_params=_cparams("parallel", "arbitrary"),
        name="norm_glu",
    )(x, g, w, w, b, b)


def _norm_swiglu_kernel(x_ref, g_ref, wg_ref, wu_ref, o_ref, h_ref):
    @pl.when(pl.program_id(1) == 0)
    def _():
        h_ref[...] = _rms(x_ref[...], g_ref[...]).astype(BF16)

    h = h_ref[...]
    a = _dot(h, wg_ref[...])
    b = _dot(h, wu_ref[...])
    o_ref[...] = (_silu(a) * b).astype(o_ref.dtype)


def _norm_swiglu(x, g, wg, wu):
    m, k = x.shape
    n = wg.shape[1]
    bm, bn = _blk(m, 1024), _blk(n, 1024)
    return pl.pallas_call(
        _norm_swiglu_kernel,
        out_shape=jax.ShapeDtypeStruct((m, n), BF16),
        grid=(m // bm, n // bn),
        in_specs=[
            pl.BlockSpec((bm, k), lambda i, j: (i, 0)),
            pl.BlockSpec((1, k), lambda i, j: (0, 0)),
            pl.BlockSpec((k, bn), lambda i, j: (0, j)),
            pl.BlockSpec((k, bn), lambda i, j: (0, j)),
        ],
        out_specs=pl.BlockSpec((bm, bn), lambda i, j: (i, j)),
        scratch_shapes=[pltpu.VMEM((bm, k), BF16)],
        compiler_params=_cparams("parallel", "arbitrary"),
        name="norm_swiglu",
    )(x, g, wg, wu)


def _norm_inproj_kernel(x_ref, g_ref, w_ref, wdt_ref, o_ref, dt_ref, h_ref):
    @pl.when(pl.program_id(1) == 0)
    def _():
        h = _rms(x_ref[...], g_ref[...]).astype(BF16)
        h_ref[...] = h
        dt_ref[...] = _dot(h, wdt_ref[...])

    o_ref[...] = _dot(h_ref[...], w_ref[...])


def _norm_inproj(x, g, w, wdt, n):
    m, k = x.shape
    bm, bn = _blk(m, 1024), _blk(n, 1024)
    return pl.pallas_call(
        _norm_inproj_kernel,
        out_shape=(jax.ShapeDtypeStruct((m, n), F32),
                   jax.ShapeDtypeStruct((m, wdt.shape[1]), F32)),
        grid=(m // bm, n // bn),
        in_specs=[
            pl.BlockSpec((bm, k), lambda i, j: (i, 0)),
            pl.BlockSpec((1, k), lambda i, j: (0, 0)),
            pl.BlockSpec((k, bn), lambda i, j: (0, j)),
            pl.BlockSpec((k, wdt.shape[1]), lambda i, j: (0, 0)),
        ],
        out_specs=(pl.BlockSpec((bm, bn), lambda i, j: (i, j)),
                   pl.BlockSpec((bm, wdt.shape[1]), lambda i, j: (i, 0))),
        scratch_shapes=[pltpu.VMEM((bm, k), BF16)],
        compiler_params=_cparams("parallel", "arbitrary"),
        name="norm_inproj",
    )(x, g, w, wdt)


def _mm_res_kernel(a_ref, w_ref, b_ref, r_ref, o_ref):
    o_ref[...] = r_ref[...] + _dot(a_ref[...], w_ref[...]) + b_ref[...]


def _mm_res(a, w, bias, res):
    m, k = a.shape
    n = w.shape[1]
    if k * n * 2 <= RESIDENT_WEIGHT_BYTES:
        bm, bn = _blk(m, 512), n
    else:
        bm, bn = _blk(m, 1024), _blk(n, 256 if k > 4096 else 512)
    return pl.pallas_call(
        _mm_res_kernel,
        out_shape=jax.ShapeDtypeStruct((m, n), F32),
        grid=(m // bm, n // bn),
        in_specs=[
            pl.BlockSpec((bm, k), lambda i, j: (i, 0)),
            pl.BlockSpec((k, bn), lambda i, j: (0, j)),
            pl.BlockSpec((1, bn), lambda i, j: (0, j)),
            pl.BlockSpec((bm, bn), lambda i, j: (i, j)),
        ],
        out_specs=pl.BlockSpec((bm, bn), lambda i, j: (i, j)),
        compiler_params=_cparams("parallel", "arbitrary"),
        name="mm_res",
    )(a, w, bias, res)


def _conv_rows(taps):
    return 16 if taps > SUBLANES else 32


def _dwconv_kernel(*refs, taps, halo, bt, cb, layer_norm):
    rows = _conv_rows(taps)
    if layer_norm:
        xc_ref, xh_ref, w_ref, b_ref, lg_ref, lb_ref, o_ref, win_ref, y_ref = refs
    else:
        xc_ref, xh_ref, w_ref, b_ref, o_ref, win_ref = refs
        y_ref = None
    first = pl.program_id(1) == 0
    win_ref[0, 0:halo, :] = jnp.where(first, 0.0, xh_ref[...])
    win_ref[0, halo:halo + bt, :] = xc_ref[...]
    off = halo - (taps - 1)
    n_sh = halo + bt - SUBLANES
    for s in sorted({(off + j) % SUBLANES for j in range(taps)} - {0}):
        win_ref[s, 0:n_sh, :] = win_ref[0, s:s + n_sh, :]

    def body(r, carry):
        r0 = pl.multiple_of(r * rows, rows)
        for c in range(cb // LANES):
            cs = slice(c * LANES, (c + 1) * LANES)
            acc = jnp.broadcast_to(b_ref[:, cs], (rows, LANES))
            for j in range(taps):
                al, s = divmod(off + j, SUBLANES)
                xw = win_ref[s, pl.ds(r0 + al * SUBLANES, rows), cs]
                acc = acc + xw * w_ref[j:j + 1, cs]
            if layer_norm:
                y_ref[pl.ds(r0, rows), cs] = acc
            else:
                o_ref[pl.ds(r0, rows), cs] = _silu(acc).astype(o_ref.dtype)
        return carry

    lax.fori_loop(0, bt // rows, body, 0)

    if layer_norm:
        y = y_ref[...]
        mu = jnp.mean(y, axis=-1, keepdims=True)
        d = y - mu
        var = jnp.mean(d * d, axis=-1, keepdims=True)
        hn = d * lax.rsqrt(var + EPS) * lg_ref[...] + lb_ref[...]
        o_ref[...] = _silu(hn).astype(o_ref.dtype)


def _dwconv(x3, w, b, ln, out_dtype, c_start=0):
    bsz, seq, _ = x3.shape
    taps, ch = w.shape
    halo = SUBLANES * (-(-(taps - 1) // SUBLANES))
    bt = _blk(seq, 256)
    assert bt % halo == 0 and bt % _conv_rows(taps) == 0
    layer_norm = ln is not None
    cb = ch if layer_norm else _blk(ch, 1024)
    assert c_start % cb == 0
    co = c_start // cb
    hb = bt // halo
    in_specs = [
        pl.BlockSpec((None, bt, cb), lambda bi, i, c: (bi, i, c + co)),
        pl.BlockSpec((None, halo, cb),
                     lambda bi, i, c: (bi, jnp.maximum(i * hb - 1, 0), c + co)),
        pl.BlockSpec((taps, cb), lambda bi, i, c: (0, c)),
        pl.BlockSpec((1, cb), lambda bi, i, c: (0, c)),
    ]
    args = [x3, x3, w, b]
    scratch = [pltpu.VMEM((SUBLANES, halo + bt, cb), F32)]
    if layer_norm:
        in_specs += [pl.BlockSpec((1, cb), lambda bi, i, c: (0, c))] * 2
        args += [ln[0], ln[1]]
        scratch.append(pltpu.VMEM((bt, cb), F32))
    return pl.pallas_call(
        functools.partial(_dwconv_kernel, taps=taps, halo=halo, bt=bt, cb=cb,
                          layer_norm=layer_norm),
        out_shape=jax.ShapeDtypeStruct((bsz, seq, ch), out_dtype),
        grid=(bsz, seq // bt, ch // cb),
        in_specs=in_specs,
        out_specs=pl.BlockSpec((None, bt, cb), lambda bi, i, c: (bi, i, c)),
        scratch_shapes=scratch,
        compiler_params=_cparams("parallel", "parallel", "parallel"),
        name="dwconv_ln" if layer_norm else "dwconv",
    )(*args)


SSD_ROWS = 64


def _split3(v):
    hi = v.astype(BF16)
    r = v - hi.astype(F32)
    mid = r.astype(BF16)
    lo = (r - mid.astype(F32)).astype(BF16)
    return hi, mid, lo


def _expand_heads(v, hpg, hd):
    rows = v.shape[0]
    lane_head = lax.broadcasted_iota(jnp.int32, (rows, hpg * hd), 1) // hd
    out = jnp.zeros((rows, hpg * hd), F32)
    for h in range(hpg):
        out = jnp.where(lane_head == h, v[:, h:h + 1], out)
    return out


def _ssd_decay_kernel(dtt_ref, dtb_ref, alog_ref, dt_ref, acst_ref, acs_ref, *, hpg, q):
    v = dtt_ref[...] + dtb_ref[...]
    dt = jnp.maximum(v, 0.0) + jnp.log1p(jnp.exp(-jnp.abs(v)))
    a = dt * (-LOG2_E * jnp.exp(alog_ref[...]))
    dt_ref[...] = dt

    rr = lax.broadcasted_iota(jnp.int32, (q, q), 0)
    cc = lax.broadcasted_iota(jnp.int32, (q, q), 1)
    tri_ls = jnp.where(rr >= cc, 1.0, 0.0).astype(BF16)
    tri_sl = jnp.where(rr <= cc, 1.0, 0.0).astype(BF16)
    acs_t = jnp.zeros(a.shape, F32)
    for piece in _split3(a):
        acs_t = acs_t + _dot(piece, tri_sl)
    acst_ref[...] = acs_t
    heads = a.shape[0]
    a_pad = jnp.concatenate([a, jnp.zeros((LANES - heads, q), F32)], axis=0) if heads < LANES else a
    acs = jnp.zeros((q, LANES), F32)
    for piece in _split3(a_pad):
        acs = acs + _dot_nt(tri_ls, piece)
    for g in range(heads // hpg):
        rolled = acs if g == 0 else pltpu.roll(acs, LANES - g * hpg, axis=1)
        acs_ref[:, g * LANES:(g + 1) * LANES] = rolled


def _ssd_decay(dtt, dt_bias, a_log, *, seq):
    heads, n_tok = dtt.shape
    hpg = heads // SSM_GROUPS
    q = _blk(seq, SSM_CHUNK)
    return pl.pallas_call(
        functools.partial(_ssd_decay_kernel, hpg=hpg, q=q),
        out_shape=(jax.ShapeDtypeStruct((heads, n_tok), F32),
                   jax.ShapeDtypeStruct((heads, n_tok), F32),
                   jax.ShapeDtypeStruct((n_tok, SSM_GROUPS * LANES), F32)),
        grid=(n_tok // q,),
        in_specs=[
            pl.BlockSpec((heads, q), lambda r: (0, r)),
            pl.BlockSpec((heads, 1), lambda r: (0, 0)),
            pl.BlockSpec((heads, 1), lambda r: (0, 0)),
        ],
        out_specs=(pl.BlockSpec((heads, q), lambda r: (0, r)),
                   pl.BlockSpec((heads, q), lambda r: (0, r)),
                   pl.BlockSpec((q, SSM_GROUPS * LANES), lambda r: (r, 0))),
        compiler_params=_cparams("parallel"),
        name="ssd_decay",
    )(dtt, dt_bias, a_log)


def _ssd_kernel(x_ref, b_ref, c_ref, z_ref, dt_ref, acst_ref, acs_ref, dexp_ref, ng_ref,
                o_ref, s_ref, y_ref, *, hpg, hd, q):
    @pl.when(pl.program_id(2) == 0)
    def _():
        s_ref[...] = jnp.zeros_like(s_ref)

    xg = x_ref[...]
    bg = b_ref[...]
    cgb = c_ref[...].astype(BF16)
    dt = dt_ref[...]
    acs_t = acst_ref[...]
    acs = acs_ref[...]
    wst_t = jnp.exp2(acs_t[:, q - 1:q] - acs_t) * dt

    cb = _dot_nt(cgb, bg.astype(BF16))
    bt = bg.T
    cbb = cb.astype(BF16)
    dtb = dt.astype(BF16)
    pair = LANES // hd
    lane_y = lax.broadcasted_iota(jnp.int32, (SSD_ROWS, LANES), 1)
    lane_s = lax.broadcasted_iota(jnp.int32, (bt.shape[0], LANES), 1)
    s_old = s_ref[...]
    for t in range(hpg // pair):
        ts = slice(t * LANES, (t + 1) * LANES)
        x_t = xg[:, ts].astype(BF16)
        s_t = None
        for p in range(pair):
            h = t * pair + p
            s_h = _dot((bt * wst_t[h:h + 1, :]).astype(BF16), x_t)
            s_t = s_h if p == 0 else jnp.where(lane_s >= p * hd, s_h, s_t)
        for rc in range(q // SSD_ROWS):
            rs = slice(rc * SSD_ROWS, (rc + 1) * SSD_ROWS)
            kc = min(q, LANES * (-(-(rc + 1) * SSD_ROWS // LANES)))
            causal_c = (lax.broadcasted_iota(jnp.int32, (SSD_ROWS, kc), 0) + rc * SSD_ROWS
                        >= lax.broadcasted_iota(jnp.int32, (SSD_ROWS, kc), 1))
            y_c = None
            for p in range(pair):
                h = t * pair + p
                seg = acs[rs, h:h + 1] - acs_t[h:h + 1, :kc]
                decay = jnp.exp2(jnp.where(causal_c, seg, -jnp.inf))
                m = cbb[rs, :kc] * decay.astype(BF16) * dtb[h:h + 1, :kc]
                y_h = _dot(m, x_t[:kc])
                y_c = y_h if p == 0 else jnp.where(lane_y >= p * hd, y_h, y_c)
            y_ref[rs, ts] = y_c
        dec_t = _expand_heads(jnp.exp2(acs[q - 1:q, t * pair:(t + 1) * pair]), pair, hd)
        s_ref[:, ts] = s_old[:, ts] * dec_t + s_t

    y = y_ref[...]
    y = y + _dot(cgb, s_old.astype(BF16)) * _expand_heads(jnp.exp2(acs[:, 0:hpg]), hpg, hd)
    y = y + xg * dexp_ref[...]
    y = y * _silu(z_ref[...])
    y = y * lax.rsqrt(jnp.mean(y * y, axis=-1, keepdims=True) + EPS) * ng_ref[...]
    o_ref[...] = y.astype(o_ref.dtype)


def _ssd(zx, xbc, dt, acs_t, acs, d_exp, norm_g, *, bsz, seq, d_inner, n_state):
    groups = SSM_GROUPS
    heads = dt.shape[0]
    hpg = heads // groups
    hd = d_inner // heads
    gw = hpg * hd
    q = _blk(seq, SSM_CHUNK)
    nc = seq // q
    assert gw % LANES == 0 and LANES % hd == 0 and n_state % LANES == 0
    assert d_inner // groups == gw
    b_off = d_inner // n_state
    c_off = b_off + groups
    row = lambda b, g, c: b * nc + c
    return pl.pallas_call(
        functools.partial(_ssd_kernel, hpg=hpg, hd=hd, q=q),
        out_shape=jax.ShapeDtypeStruct((bsz * seq, d_inner), BF16),
        grid=(bsz, groups, nc),
        in_specs=[
            pl.BlockSpec((q, gw), lambda b, g, c: (row(b, g, c), g)),
            pl.BlockSpec((q, n_state), lambda b, g, c: (row(b, g, c), b_off + g)),
            pl.BlockSpec((q, n_state), lambda b, g, c: (row(b, g, c), c_off + g)),
            pl.BlockSpec((q, gw), lambda b, g, c: (row(b, g, c), g)),
            pl.BlockSpec((hpg, q), lambda b, g, c: (g, row(b, g, c))),
            pl.BlockSpec((hpg, q), lambda b, g, c: (g, row(b, g, c))),
            pl.BlockSpec((q, LANES), lambda b, g, c: (row(b, g, c), g)),
            pl.BlockSpec((1, gw), lambda b, g, c: (0, g)),
            pl.BlockSpec((1, gw), lambda b, g, c: (0, g)),
        ],
        out_specs=pl.BlockSpec((q, gw), lambda b, g, c: (row(b, g, c), g)),
        scratch_shapes=[pltpu.VMEM((n_state, gw), F32), pltpu.VMEM((q, gw), F32)],
        compiler_params=_cparams("parallel", "parallel", "arbitrary"),
        name="ssd",
    )(xbc, xbc, xbc, zx, dt, acs_t, acs, d_exp, norm_g)


def _router_kernel(x_ref, g_ref, wr_ref, idx_ref, gw_ref, *, n_experts):
    h = _rms(x_ref[...], g_ref[...])
    w = wr_ref[...]
    hh = h.astype(BF16)
    hl = (h - hh.astype(F32)).astype(BF16)
    wh = w.astype(BF16)
    wl = (w - wh.astype(F32)).astype(BF16)
    logits = _dot(hh, wh) + _dot(hl, wh) + _dot(hh, wl)
    lane = lax.broadcasted_iota(jnp.int32, logits.shape, 1).astype(F32)
    neg = -jnp.inf
    lg = jnp.where(lane < n_experts, logits, neg)
    m1 = jnp.max(lg, axis=-1, keepdims=True)
    i1 = jnp.min(jnp.where(lg == m1, lane, float(LANES)), axis=-1, keepdims=True)
    lg2 = jnp.where(lane == i1, neg, lg)
    m2 = jnp.max(lg2, axis=-1, keepdims=True)
    i2 = jnp.min(jnp.where(lg2 == m2, lane, float(LANES)), axis=-1, keepdims=True)
    e = jnp.exp(m2 - m1)
    w1 = 1.0 / (1.0 + e)
    w2 = e / (1.0 + e)
    idx_ref[...] = jnp.where(lane == 0.0, i1, i2).astype(jnp.int32)
    gw_ref[...] = jnp.where(lane == 0.0, w1, w2)


def _router(x, g, wr, n_experts):
    m, k = x.shape
    bm = _blk(m, 512)
    return pl.pallas_call(
        functools.partial(_router_kernel, n_experts=n_experts),
        out_shape=(jax.ShapeDtypeStruct((m, LANES), jnp.int32),
                   jax.ShapeDtypeStruct((m, LANES), F32)),
        grid=(m // bm,),
        in_specs=[
            pl.BlockSpec((bm, k), lambda i: (i, 0)),
            pl.BlockSpec((1, k), lambda i: (0, 0)),
            pl.BlockSpec((k, LANES), lambda i: (0, 0)),
        ],
        out_specs=(pl.BlockSpec((bm, LANES), lambda i: (i, 0)),
                   pl.BlockSpec((bm, LANES), lambda i: (i, 0))),
        compiler_params=_cparams("parallel"),
        name="router",
    )(x, g, wr)


def _rank_kernel(e_ref, rank_ref, cnt_ref, carry_ref, *, ep, bl):
    @pl.when(pl.program_id(0) == 0)
    def _():
        carry_ref[...] = jnp.zeros_like(carry_ref)

    e_row = e_ref[0]
    sub = lax.broadcasted_iota(jnp.int32, (ep, bl), 0)
    hit = sub == e_row
    oh = jnp.where(hit, 1.0, 0.0).astype(BF16)
    rr = lax.broadcasted_iota(jnp.int32, (bl, bl), 0)
    cc = lax.broadcasted_iota(jnp.int32, (bl, bl), 1)
    tri = jnp.where(rr <= cc, 1.0, 0.0).astype(BF16)
    pre = _dot(oh, tri)
    carry = carry_ref[...]
    rank = jnp.sum(jnp.where(hit, pre - 1.0 + carry[:, 0:1], 0.0), axis=0, keepdims=True)
    rank_ref[0] = rank.astype(jnp.int32)
    carry = carry + pre[:, bl - 1:bl]
    carry_ref[...] = carry
    cnt_ref[...] = carry.astype(jnp.int32)


def _ranks(e_flat, n_experts):
    n = e_flat.shape[0]
    bl = _blk(n, 512)
    ep = 2 * SUBLANES * (-(-n_experts // (2 * SUBLANES)))
    rank, cnt = pl.pallas_call(
        functools.partial(_rank_kernel, ep=ep, bl=bl),
        out_shape=(jax.ShapeDtypeStruct((n // bl, 1, bl), jnp.int32),
                   jax.ShapeDtypeStruct((ep, LANES), jnp.int32)),
        grid=(n // bl,),
        in_specs=[pl.BlockSpec((1, 1, bl), lambda i: (i, 0, 0))],
        out_specs=(pl.BlockSpec((1, 1, bl), lambda i: (i, 0, 0)),
                   pl.BlockSpec((ep, LANES), lambda i: (0, 0))),
        scratch_shapes=[pltpu.VMEM((ep, LANES), F32)],
        compiler_params=_cparams("arbitrary"),
        name="expert_ranks",
    )(e_flat.reshape(n // bl, 1, bl))
    return rank.reshape(n), cnt[:n_experts, 0]


def _row_copy(src, dst, s_row, d_row, sem):
    return pltpu.make_async_copy(src.at[pl.ds(s_row, 1)], dst.at[pl.ds(d_row, 1)], sem)


def _scatter_kernel(pos_ref, gap0_ref, gapn_ref, x_ref, g_ref, xs_ref, h_ref, z_ref, sem, zsem,
                    *, rb, n_tok, n_gaps):
    base = pl.program_id(0) * rb

    @pl.when(pl.program_id(0) == 0)
    def _():
        z_ref[...] = jnp.zeros_like(z_ref)
        for e in range(n_gaps):
            def fill(r, carry, e=e):
                _row_copy(z_ref, xs_ref, 0, gap0_ref[e] + r, zsem).start()
                return carry

            lax.fori_loop(0, gapn_ref[e], fill, 0)

    h_ref[...] = _rms(x_ref[...], g_ref[...])

    def issue(r, carry):
        for k in range(TOP_K):
            _row_copy(h_ref, xs_ref, r, pos_ref[k * n_tok + base + r], sem).start()
        return carry

    lax.fori_loop(0, rb, issue, 0, unroll=8)

    def drain(r, carry):
        for k in range(TOP_K):
            _row_copy(h_ref, xs_ref, 0, 0, sem).wait()
        return carry

    lax.fori_loop(0, rb, drain, 0, unroll=8)

    @pl.when(pl.program_id(0) == pl.num_programs(0) - 1)
    def _():
        for e in range(n_gaps):
            def fill_done(r, carry):
                _row_copy(z_ref, xs_ref, 0, 0, zsem).wait()
                return carry

            lax.fori_loop(0, gapn_ref[e], fill_done, 0)


def _scatter_rows(pos, gap_start, gap_len, x, g, n_rows):
    n_tok, k = x.shape
    rb = _blk(n_tok, 256)
    return pl.pallas_call(
        functools.partial(_scatter_kernel, rb=rb, n_tok=n_tok, n_gaps=gap_start.shape[0]),
        out_shape=jax.ShapeDtypeStruct((n_rows, k), F32),
        grid_spec=pltpu.PrefetchScalarGridSpec(
            num_scalar_prefetch=3,
            grid=(n_tok // rb,),
            in_specs=[pl.BlockSpec((rb, k), lambda i, *_: (i, 0)),
                      pl.BlockSpec((1, k), lambda i, *_: (0, 0))],
            out_specs=pl.BlockSpec(memory_space=pl.ANY),
            scratch_shapes=[pltpu.VMEM((rb, k), F32), pltpu.VMEM((SUBLANES, k), F32),
                            pltpu.SemaphoreType.DMA, pltpu.SemaphoreType.DMA],
        ),
        compiler_params=_cparams("arbitrary"),
        name="moe_scatter",
    )(pos, gap_start, gap_len, x, g)


N_TILE_TABLES = 6


def _tile_tables(te, nu):
    first = jnp.concatenate([jnp.ones((1,), jnp.int32), (te[1:] != te[:-1]).astype(jnp.int32)])
    run = jnp.cumsum(first) - 1
    later = jnp.where(te[None, :] > te[:, None], te[None, :], jnp.iinfo(jnp.int32).max)
    nxt = jnp.min(later, axis=1)
    nxt = jnp.where(nxt == jnp.iinfo(jnp.int32).max, te[0], nxt)
    return (te, nu, first, run.astype(jnp.int32), nxt.astype(jnp.int32),
            (run[-1:] + 1).astype(jnp.int32))


def _weight_copies(w_refs, wbuf, sems, e, j, slot, bn):
    cols = pl.ds(pl.multiple_of(j * bn, bn), bn)
    return [pltpu.make_async_copy(w.at[e, :, cols], wbuf.at[slot, n], sems.at[slot, n])
            for n, w in enumerate(w_refs)]


def _run_weights(tables, w_refs, wbuf, sems, bn):
    te_ref, _, first_ref, run_ref, nxt_ref, nruns_ref = tables
    j, i = pl.program_id(0), pl.program_id(1)
    n_runs = nruns_ref[0]
    g = j * n_runs + run_ref[i]
    slot = lax.rem(g, 2)

    @pl.when(first_ref[i] == 1)
    def _():
        @pl.when(g == 0)
        def _():
            for c in _weight_copies(w_refs, wbuf, sems, te_ref[0], 0, 0, bn):
                c.start()

        for c in _weight_copies(w_refs, wbuf, sems, te_ref[i], j, slot, bn):
            c.wait()
        last_run = run_ref[i] == n_runs - 1

        @pl.when(jnp.logical_not(jnp.logical_and(last_run, j == pl.num_programs(0) - 1)))
        def _():
            nj = j + last_run.astype(jnp.int32)
            for c in _weight_copies(w_refs, wbuf, sems, nxt_ref[i], nj, 1 - slot, bn):
                c.start()

    return slot


def _moe_up_kernel(*refs, bn):
    tables = refs[:N_TILE_TABLES]
    xs_ref, wg_ref, wu_ref, o_ref, wbuf, sems = refs[N_TILE_TABLES:]
    slot = _run_weights(tables, (wg_ref, wu_ref), wbuf, sems, bn)

    @pl.when(pl.program_id(1) < tables[1][0])
    def _():
        h = xs_ref[...].astype(BF16)
        for c in range(bn // MXU_COLS):
            cs = slice(c * MXU_COLS, (c + 1) * MXU_COLS)
            a = _dot(h, wbuf[slot, 0, :, cs].astype(BF16))
            b = _dot(h, wbuf[slot, 1, :, cs].astype(BF16))
            o_ref[:, cs] = (_silu(a) * b).astype(o_ref.dtype)

    @pl.when(pl.program_id(1) >= tables[1][0])
    def _():
        o_ref[...] = jnp.zeros_like(o_ref)


def _moe_up(tables, xs, wg, wu, bm):
    rows, k = xs.shape
    f = wg.shape[2]
    bn = _blk(f, 1024)
    return pl.pallas_call(
        functools.partial(_moe_up_kernel, bn=bn),
        out_shape=jax.ShapeDtypeStruct((rows, f), BF16),
        grid_spec=pltpu.PrefetchScalarGridSpec(
            num_scalar_prefetch=N_TILE_TABLES,
            grid=(f // bn, rows // bm),
            in_specs=[
                pl.BlockSpec((bm, k), lambda j, i, *_: (i, 0)),
                pl.BlockSpec(memory_space=pl.ANY),
                pl.BlockSpec(memory_space=pl.ANY),
            ],
            out_specs=pl.BlockSpec((bm, bn), lambda j, i, *_: (i, j)),
            scratch_shapes=[pltpu.VMEM((2, 2, k, bn), F32), pltpu.SemaphoreType.DMA((2, 2))],
        ),
        compiler_params=_cparams("arbitrary", "arbitrary"),
        name="moe_up",
    )(*tables, xs, wg, wu)


def _moe_down_kernel(*refs, bn):
    tables = refs[:N_TILE_TABLES]
    a_ref, wd_ref, o_ref, wbuf, sems = refs[N_TILE_TABLES:]
    slot = _run_weights(tables, (wd_ref,), wbuf, sems, bn)

    @pl.when(pl.program_id(1) < tables[1][0])
    def _():
        a = a_ref[...]
        for c in range(bn // MXU_COLS):
            cs = slice(c * MXU_COLS, (c + 1) * MXU_COLS)
            o_ref[:, cs] = _dot(a, wbuf[slot, 0, :, cs].astype(BF16))

    @pl.when(pl.program_id(1) >= tables[1][0])
    def _():
        o_ref[...] = jnp.zeros_like(o_ref)


def _moe_down(tables, act, wd, bm):
    rows, f = act.shape
    n = wd.shape[2]
    bn = _blk(n, 512)
    return pl.pallas_call(
        functools.partial(_moe_down_kernel, bn=bn),
        out_shape=jax.ShapeDtypeStruct((rows, n), F32),
        grid_spec=pltpu.PrefetchScalarGridSpec(
            num_scalar_prefetch=N_TILE_TABLES,
            grid=(n // bn, rows // bm),
            in_specs=[
                pl.BlockSpec((bm, f), lambda j, i, *_: (i, 0)),
                pl.BlockSpec(memory_space=pl.ANY),
            ],
            out_specs=pl.BlockSpec((bm, bn), lambda j, i, *_: (i, j)),
            scratch_shapes=[pltpu.VMEM((2, 1, f, bn), F32), pltpu.SemaphoreType.DMA((2, 1))],
        ),
        compiler_params=_cparams("arbitrary", "arbitrary"),
        name="moe_down",
    )(*tables, act, wd)


def _combine_kernel(pos_ref, x_ref, gw_ref, fg_ref, ys_ref, o_ref, buf_ref, sems, *, rb, n_tok):
    i = pl.program_id(0)
    slot = lax.rem(i, 2)

    def gather(block, to_slot):
        def issue(r, carry):
            for k in range(TOP_K):
                _row_copy(ys_ref, buf_ref.at[to_slot, k], pos_ref[k * n_tok + block * rb + r], r,
                          sems.at[to_slot]).start()
            return carry

        lax.fori_loop(0, rb, issue, 0, unroll=8)

    @pl.when(i == 0)
    def _():
        gather(0, 0)

    @pl.when(i + 1 < pl.num_programs(0))
    def _():
        gather(i + 1, 1 - slot)

    def drain(r, carry):
        for k in range(TOP_K):
            _row_copy(ys_ref, buf_ref.at[slot, k], 0, 0, sems.at[slot]).wait()
        return carry

    lax.fori_loop(0, rb, drain, 0, unroll=8)

    gw = gw_ref[...]
    y = x_ref[...] + gw[:, 0:1] * buf_ref[slot, 0] + gw[:, 1:2] * buf_ref[slot, 1]
    o_ref[...] = _rms(y, fg_ref[...])


def _combine_norm(pos, x, gw, fg, ys):
    n_tok, d = x.shape
    rb = _blk(n_tok, 256)
    return pl.pallas_call(
        functools.partial(_combine_kernel, rb=rb, n_tok=n_tok),
        out_shape=jax.ShapeDtypeStruct((n_tok, d), F32),
        grid_spec=pltpu.PrefetchScalarGridSpec(
            num_scalar_prefetch=1,
            grid=(n_tok // rb,),
            in_specs=[
                pl.BlockSpec((rb, d), lambda i, pos: (i, 0)),
                pl.BlockSpec((rb, LANES), lambda i, pos: (i, 0)),
                pl.BlockSpec((1, d), lambda i, pos: (0, 0)),
                pl.BlockSpec(memory_space=pl.ANY),
            ],
            out_specs=pl.BlockSpec((rb, d), lambda i, pos: (i, 0)),
            scratch_shapes=[pltpu.VMEM((2, TOP_K, rb, d), F32), pltpu.SemaphoreType.DMA((2,))],
        ),
        compiler_params=_cparams("arbitrary"),
        name="moe_combine_norm",
    )(pos, x, gw, fg, ys)


def _moe_block(x, norm_g, final_g, w_router, w_gate, w_up, w_down):
    n_tok, d = x.shape
    n_experts = w_router.shape[1]
    bm = _blk(n_tok, 512)
    wr = jnp.zeros((d, LANES), F32).at[:, :n_experts].set(w_router)
    idx, gw = _router(x, norm_g, wr, n_experts)

    e_flat = idx[:, :TOP_K].T.reshape(TOP_K * n_tok)
    rank, counts = _ranks(e_flat, n_experts)

    tiles = (counts + bm - 1) // bm
    tile_end = jnp.cumsum(tiles)
    row_off = (tile_end - tiles) * bm
    pos = row_off[e_flat] + rank
    n_tiles = TOP_K * n_tok // bm + n_experts
    n_used = tile_end[-1]
    tile_ids = jnp.minimum(jnp.arange(n_tiles, dtype=jnp.int32), n_used - 1)
    te = jnp.sum(tile_ids[:, None] >= tile_end[None, :], axis=1).astype(jnp.int32)
    nu = n_used.reshape(1).astype(jnp.int32)

    gap_start = jnp.concatenate([row_off + counts, n_used[None] * bm]).astype(jnp.int32)
    gap_len = jnp.concatenate([tiles * bm - counts, (n_tiles - n_used)[None] * bm]).astype(jnp.int32)
    xs = _scatter_rows(pos, gap_start, gap_len, x, norm_g, n_tiles * bm)
    tables = _tile_tables(te, nu)
    act = _moe_up(tables, xs, w_gate, w_up, bm)
    ys = _moe_down(tables, act, w_down, bm)
    return _combine_norm(pos, x, gw, final_g, ys)


def _conformer_layer(x, bsz, seq, mix_g, ffn_g, pw1_w, pw1_b, dw_w, dw_b, ln_g, ln_b,
                     pw2_w, pw2_b, w_gate, w_up, w_down):
    d = x.shape[1]
    row = lambda v: v.reshape(1, -1)
    glu = _norm_glu(x, row(mix_g), pw1_w.astype(BF16), row(pw1_b), F32)
    hc = _dwconv(glu.reshape(bsz, seq, d), dw_w, row(dw_b), (row(ln_g), row(ln_b)), BF16)
    x = _mm_res(hc.reshape(bsz * seq, d), pw2_w.astype(BF16), row(pw2_b), x)
    act = _norm_swiglu(x, row(ffn_g), w_gate.astype(BF16), w_up.astype(BF16))
    return _mm_res(act, w_down.astype(BF16), jnp.zeros((1, d), F32), x)


def _mamba_mixer(x, bsz, seq, mix_g, w_in, conv_w, conv_b, dt_bias, a_log, d_skip, norm_g, w_out):
    d = x.shape[1]
    heads = dt_bias.shape[0]
    d_inner = norm_g.shape[0]
    conv_dim = conv_w.shape[1]
    n_state = (conv_dim - d_inner) // (2 * SSM_GROUPS)
    hd = d_inner // heads
    row = lambda v: v.reshape(1, -1)
    n_main = d_inner + conv_dim
    w_dt = jnp.zeros((d, LANES), BF16).at[:, :heads].set(w_in[:, n_main:].astype(BF16))
    zx, dt_raw = _norm_inproj(x, row(mix_g), w_in.astype(BF16), w_dt, n_main)
    xbc = _dwconv(zx.reshape(bsz, seq, n_main), conv_w, row(conv_b), None, F32, c_start=d_inner)
    dt, acs_t, acs = _ssd_decay(dt_raw[:, :heads].T, dt_bias.reshape(heads, 1),
                                a_log.reshape(heads, 1), seq=seq)
    y = _ssd(zx, xbc.reshape(bsz * seq, conv_dim), dt, acs_t, acs,
             row(jnp.repeat(d_skip, hd)), row(norm_g),
             bsz=bsz, seq=seq, d_inner=d_inner, n_state=n_state)
    return _mm_res(y, w_out.astype(BF16), jnp.zeros((1, d), F32), x)


def kernel(x, norm_mix_g, norm_ffn_g, final_norm_g, conf_pw1_w, conf_pw1_b, conf_dw_w, conf_dw_b, conf_ln_g, conf_ln_b, conf_pw2_w, conf_pw2_b, ssm_in_w, ssm_conv_w, ssm_conv_b, ssm_dt_bias, ssm_a_log, ssm_d, ssm_norm_g, ssm_out_w, ffn_w_gate, ffn_w_up, ffn_w_down, moe_router_w, moe_w_gate, moe_w_up, moe_w_down):
    bsz, seq, d = x.shape
    assert norm_mix_g.shape[0] == 2, "two layers: Conformer conv + SwiGLU, then Mamba-2 + MoE"
    xf = x.reshape(bsz * seq, d)
    xf = _conformer_layer(xf, bsz, seq, norm_mix_g[0], norm_ffn_g[0], conf_pw1_w[0], conf_pw1_b[0],
                          conf_dw_w[0], conf_dw_b[0], conf_ln_g[0], conf_ln_b[0], conf_pw2_w[0],
                          conf_pw2_b[0], ffn_w_gate[0], ffn_w_up[0], ffn_w_down[0])
    xf = _mamba_mixer(xf, bsz, seq, norm_mix_g[1], ssm_in_w[0], ssm_conv_w[0], ssm_conv_b[0],
                      ssm_dt_bias[0], ssm_a_log[0], ssm_d[0], ssm_norm_g[0], ssm_out_w[0])
    out = _moe_block(xf, norm_ffn_g[1].reshape(1, d), final_norm_g.reshape(1, d),
                     moe_router_w[0], moe_w_gate.reshape(moe_w_gate.shape[1:]),
                     moe_w_up.reshape(moe_w_up.shape[1:]), moe_w_down.reshape(moe_w_down.shape[1:]))
    return out.reshape(bsz, seq, d)
```
